```python
import jax
import jax.numpy as jnp
from jax import lax
import numpy as np

D_MODEL = 1024
BATCH = 16
SEQ = 2048
DEPTH = 2

GRID_W = 64
CTX_LEN = 256
HEAD_DIM = 64
N_GROUPS = 4
GROUP_HEADS = D_MODEL // HEAD_DIM // N_GROUPS
GROUP_WIDTH = GROUP_HEADS * HEAD_DIM
MIX_WIDTH = N_GROUPS * GROUP_WIDTH
GQA_KV_HEADS = GROUP_HEADS // 2
KV_WIDTH = GQA_KV_HEADS * HEAD_DIM
MLP_HIDDEN = 4 * D_MODEL
CHUNK = 64
Q_BLOCK = 128
NA_KH = 8
NA_KW = 16
ROPE_THETA = 10000.0
EPS = 1e-6
NEG_BIG = -1e30
FORGET_FLOOR = 1e-20
IN_SPLITS = (
    GROUP_WIDTH, GROUP_WIDTH, GROUP_WIDTH, GROUP_WIDTH, GROUP_WIDTH,
    GROUP_WIDTH, GROUP_WIDTH, GROUP_WIDTH, GROUP_WIDTH, 4 * GROUP_HEADS,
    GROUP_WIDTH, KV_WIDTH, KV_WIDTH,
    GROUP_WIDTH, GROUP_WIDTH, GROUP_WIDTH,
)
IN_WIDTH = sum(IN_SPLITS)
F32 = jnp.float32

kernel_name = "hybrid_parallel_group_diffusion_block"


def _rmsnorm(t, g):
    tf = t.astype(F32)
    y = tf * lax.rsqrt(jnp.mean(tf * tf, axis=-1, keepdims=True) + EPS)
    return (y * g.astype(F32)).astype(t.dtype)


def _modulate(h, shift, scale):
    return h * (1 + scale) + shift


def _split_in(z):
    idx = [int(o) for o in np.cumsum(IN_SPLITS)[:-1]]
    return jnp.split(z, idx, axis=-1)


def _heads(t, nh):
    b, s, _ = t.shape
    return t.reshape(b, s, nh, -1).transpose(0, 2, 1, 3)


def _merge(t):
    b, h, s, e = t.shape
    return t.transpose(0, 2, 1, 3).reshape(b, s, h * e)


def _to_chunks(t):
    b, h, s, e = t.shape
    return t.reshape(b, h, s // CHUNK, CHUNK, e).transpose(2, 0, 1, 3, 4)


def _from_chunks(t):
    nc, b, h, l, e = t.shape
    return t.transpose(1, 2, 0, 3, 4).reshape(b, h, nc * l, e)


def _axial_rope(n_tokens, dtype):
    t = jnp.arange(n_tokens)
    row = (t // GRID_W).astype(F32)
    col = (t % GRID_W).astype(F32)
    axis_dims = HEAD_DIM // 2
    inv = jnp.power(ROPE_THETA, -2.0 * jnp.arange(axis_dims // 2, dtype=F32) / axis_dims)
    ang = jnp.concatenate([row[:, None] * inv, col[:, None] * inv], axis=-1)
    return jnp.cos(ang).astype(dtype), jnp.sin(ang).astype(dtype)


def _apply_rope(t, cos, sin):
    t2 = t.reshape(*t.shape[:-1], t.shape[-1] // 2, 2)
    t0, t1 = t2[..., 0], t2[..., 1]
    return jnp.stack([t0 * cos - t1 * sin, t0 * sin + t1 * cos], axis=-1).reshape(t.shape)


def _hgrn2_inputs(q, i, f, lb):
    q = _heads(jax.nn.silu(q), GROUP_HEADS).astype(F32) * HEAD_DIM ** -0.5
    v = _heads(i, GROUP_HEADS).astype(F32)
    f = _heads(f, GROUP_HEADS).astype(F32)
    lbh = lb.reshape(GROUP_HEADS, 1, HEAD_DIM).astype(F32)
    forget = lbh + (1 - lbh) * jax.nn.sigmoid(f)
    log_forget = jnp.log(jnp.maximum(forget, FORGET_FLOOR))
    k = (1 - lbh) * jax.nn.sigmoid(-f)
    return (q, k, v, log_forget)


def _hgrn2_scan(inputs, state):
    tri = jnp.tril(jnp.ones((CHUNK, CHUNK), bool))

    def step(s, blk):
        qc, kc, vc, gc = blk
        b = jnp.cumsum(gc, axis=2)
        rel = jnp.where(tri[:, :, None], b[:, :, :, None, :] - b[:, :, None, :, :], NEG_BIG)
        a = jnp.einsum('bhtd,bhsd,bhtsd->bhts', qc, kc, jnp.exp(rel))
        o = jnp.einsum('bhts,bhsv->bhtv', a, vc) + jnp.einsum('bhtd,bhdv->bhtv', qc * jnp.exp(b), s)
        b_end = b[:, :, -1, :]
        s_new = jnp.exp(b_end)[..., None] * s + jnp.einsum(
            'bhsd,bhsv->bhdv', kc * jnp.exp(b_end[:, :, None, :] - b), vc)
        return s_new, o

    s_fin, o = lax.scan(step, state, tuple(_to_chunks(t) for t in inputs))
    return _from_chunks(o), s_fin


def _mlstm_inputs(q, k, v, gates, gate_b):
    b, s, _ = gates.shape
    q = _heads(q, GROUP_HEADS).astype(F32)
    k = _heads(k, GROUP_HEADS).astype(F32) * HEAD_DIM ** -0.5
    v = _heads(v, GROUP_HEADS).astype(F32)
    g = (gates.astype(F32) + gate_b.astype(F32)).reshape(b, s, 4, GROUP_HEADS).transpose(2, 0, 3, 1)[..., None]
    ig_f, ig_b, fg_f, fg_b = g[0], g[1], g[2], g[3]
    return ((q, k, v, ig_f, jax.nn.log_sigmoid(fg_f)), (q, k, v, ig_b, jax.nn.log_sigmoid(fg_b)))


def _mlstm_scan(inputs, state):
    tri = jnp.tril(jnp.ones((CHUNK, CHUNK), bool))

    def step(carry, blk):
        cmat, nvec, m = carry
        qc, kc, vc, igc, lfc = blk
        igc, lfc = igc[..., 0], lfc[..., 0]
        b = jnp.cumsum(lfc, axis=-1)
        a = b + m[..., None]
        dlog = jnp.where(tri, b[..., :, None] - b[..., None, :] + igc[..., None, :], NEG_BIG)
        m_t = jnp.maximum(a, jnp.max(dlog, axis=-1))
        w_in = jnp.exp(a - m_t)
        p = jnp.exp(dlog - m_t[..., None]) * jnp.einsum('bhtd,bhsd->bhts', qc, kc)
        num = w_in[..., None] * jnp.einsum('bhtd,bhdv->bhtv', qc, cmat) + jnp.einsum('bhts,bhsv->bhtv', p, vc)
        den = w_in * jnp.einsum('bhtd,bhd->bht', qc, nvec) + jnp.sum(p, axis=-1)
        h = num / jnp.maximum(jnp.abs(den), jnp.exp(-m_t))[..., None]
        b_end = b[..., -1]
        g_s = b_end[..., None] - b + igc
        m_new = jnp.maximum(b_end + m, jnp.max(g_s, axis=-1))
        w_old = jnp.exp(b_end + m - m_new)
        w_s = jnp.exp(g_s - m_new[..., None])
        c_new = w_old[..., None, None] * cmat + jnp.einsum('bhs,bhsd,bhsv->bhdv', w_s, kc, vc)
        n_new = w_old[..., None] * nvec + jnp.einsum('bhs,bhsd->bhd', w_s, kc)
        return (c_new, n_new, m_new), h

    s_fin, h = lax.scan(step, state, tuple(_to_chunks(t) for t in inputs))
    return _from_chunks(h), s_fin


def _bidir_prefix(scan_fn, init_state, ctx_f, ctx_b, lat_f, lat_b):
    flip = lambda ts: tuple(jnp.flip(t, axis=2) for t in ts)
    oc_f, sc_f = scan_fn(ctx_f, init_state)
    ol_f, _ = scan_fn(lat_f, sc_f)
    oc_b, sc_b = scan_fn(flip(ctx_b), init_state)
    ol_b, _ = scan_fn(flip(lat_b), sc_b)
    return oc_f + jnp.flip(oc_b, axis=2), ol_f + jnp.flip(ol_b, axis=2)


def _gated_head_norm(o, g, gate):
    return _merge(_rmsnorm(o, g)).astype(gate.dtype) * gate


def _gqa_heads(q, k, v, qn_g, kn_g):
    return (_rmsnorm(_heads(q, GROUP_HEADS), qn_g), _rmsnorm(_heads(k, GQA_KV_HEADS), kn_g),
            _heads(v, GQA_KV_HEADS))


def _gqa_latent(q, k, v, k_ctx, v_ctx):
    b, hq, s, d = q.shape
    hkv = k.shape[1]
    kk = jnp.concatenate([k_ctx, k], axis=2)
    vv = jnp.concatenate([v_ctx, v], axis=2)
    qb = q.reshape(b, hkv, hq // hkv, s // Q_BLOCK, Q_BLOCK, d).transpose(3, 0, 1, 2, 4, 5)

    def block(qi):
        sc = jnp.einsum('bkgqd,bksd->bkgqs', qi, kk).astype(F32) * d ** -0.5
        return jnp.einsum('bkgqs,bksd->bkgqd', jax.nn.softmax(sc, axis=-1).astype(vv.dtype), vv)

    o = lax.map(block, qb)
    return o.transpose(1, 2, 3, 0, 4, 5).reshape(b, hq, s, d)


def _dense_attn(q, k, v):
    b, hq, s, d = q.shape
    hkv = k.shape[1]
    qg = q.reshape(b, hkv, hq // hkv, s, d)
    sc = jnp.einsum('bkgqd,bksd->bkgqs', qg, k).astype(F32) * d ** -0.5
    o = jnp.einsum('bkgqs,bksd->bkgqd', jax.nn.softmax(sc, axis=-1).astype(v.dtype), v)
    return o.reshape(b, hq, s, d)


def _neighbourhood_attn(q, k, v, k_ctx, v_ctx, rpb):
    b, h, s, d = q.shape
    rows = s // GRID_W
    kh = min(NA_KH, rows)
    scale = d ** -0.5
    qg = q.reshape(b, h, rows, GRID_W, d)
    kg = k.reshape(b, h, rows, GRID_W, d)
    vg = v.reshape(b, h, rows, GRID_W, d)
    col = jnp.arange(GRID_W)
    cstart = jnp.clip(col - NA_KW // 2, 0, GRID_W - NA_KW)
    col_in = (col[None, :] >= cstart[:, None]) & (col[None, :] < cstart[:, None] + NA_KW)
    col_idx = jnp.clip(col[None, :] - col[:, None], 1 - NA_KW, NA_KW - 1) + NA_KW - 1
    mask = jnp.broadcast_to(col_in[:, None, :], (GRID_W, kh, GRID_W)).reshape(GRID_W, kh * GRID_W)
    rpb_cols = rpb[:, :, col_idx]

    def row_block(r):
        r0 = jnp.clip(r - kh // 2, 0, rows - kh)
        kr = lax.dynamic_slice_in_dim(kg, r0, kh, axis=2).reshape(b, h, kh * GRID_W, d)
        vr = lax.dynamic_slice_in_dim(vg, r0, kh, axis=2).reshape(b, h, kh * GRID_W, d)
        qr = lax.dynamic_index_in_dim(qg, r, axis=2, keepdims=False)
        row_idx = r0 + jnp.arange(kh) - r + NA_KH - 1
        bias = jnp.take(rpb_cols, row_idx, axis=1).transpose(0, 2, 1, 3).reshape(h, GRID_W, kh * GRID_W)
        s_loc = jnp.einsum('bhqd,bhkd->bhqk', qr, kr).astype(F32) * scale + bias.astype(F32)
        s_loc = jnp.where(mask, s_loc, NEG_BIG)
        s_ctx = jnp.einsum('bhqd,bhcd->bhqc', qr, k_ctx).astype(F32) * scale
        p = jax.nn.softmax(jnp.concatenate([s_ctx, s_loc], axis=-1), axis=-1).astype(v.dtype)
        return jnp.einsum('bhqk,bhkd->bhqd', p, jnp.concatenate([v_ctx, vr], axis=2))

    o = lax.map(row_block, jnp.arange(rows))
    return o.transpose(1, 2, 0, 3, 4).reshape(b, h, s, d)


def _sqrelu_mlp(h, w1, w2):
    return jnp.square(jax.nn.relu(h @ w1)) @ w2


def _layer(x, xc, c, c_ctx, w_mod, b_mod, g1, g2, w_in, lb, hgrn_g, mlstm_b, mlstm_g,
           qn_g, kn_g, rpb, w_out, w1, w2, rope, need_ctx):
    dt = x.dtype
    mod = jnp.split((jax.nn.silu(c) @ w_mod + b_mod)[:, None, :], 6, axis=-1)
    modc = jnp.split((jax.nn.silu(c_ctx) @ w_mod + b_mod)[None, None, :], 6, axis=-1)
    zl = _split_in(_modulate(_rmsnorm(x, g1), mod[0], mod[1]) @ w_in)
    zc = _split_in(_modulate(_rmsnorm(xc, g1), modc[0], modc[1]) @ w_in)
    bsz = x.shape[0]

    a_c, a_l = _bidir_prefix(
        _hgrn2_scan, jnp.zeros((bsz, GROUP_HEADS, HEAD_DIM, HEAD_DIM), F32),
        _hgrn2_inputs(zc[0], zc[1], zc[3], lb[0]), _hgrn2_inputs(zc[0], zc[1], zc[4], lb[1]),
        _hgrn2_inputs(zl[0], zl[1], zl[3], lb[0]), _hgrn2_inputs(zl[0], zl[1], zl[4], lb[1]))
    a_l = _gated_head_norm(a_l, hgrn_g, jax.nn.silu(zl[2]))

    m_cf, m_cb = _mlstm_inputs(zc[5], zc[6], zc[7], zc[9], mlstm_b)
    m_lf, m_lb = _mlstm_inputs(zl[5], zl[6], zl[7], zl[9], mlstm_b)
    st0 = (jnp.zeros((bsz, GROUP_HEADS, HEAD_DIM, HEAD_DIM), F32),
           jnp.zeros((bsz, GROUP_HEADS, HEAD_DIM), F32), jnp.zeros((bsz, GROUP_HEADS), F32))
    b_c, b_l = _bidir_prefix(_mlstm_scan, st0, m_cf, m_cb, m_lf, m_lb)
    b_l = _gated_head_norm(b_l, mlstm_g, jax.nn.sigmoid(zl[8]))

    ql, kl, vl = _gqa_heads(zl[10], zl[11], zl[12], qn_g, kn_g)
    qc_, kc_, vc_ = _gqa_heads(zc[10], zc[11], zc[12], qn_g, kn_g)
    c_l = _merge(_gqa_latent(_apply_rope(ql, *rope), _apply_rope(kl, *rope), vl, kc_, vc_))

    nq, nk, nv = (_heads(t, GROUP_HEADS) for t in zl[13:16])
    cq, ck, cv = (_heads(t, GROUP_HEADS) for t in zc[13:16])
    d_l = _merge(_neighbourhood_attn(nq, nk, nv, ck, cv, rpb))

    y = jnp.concatenate([a_l.astype(dt), b_l.astype(dt), c_l.astype(dt), d_l.astype(dt)], axis=-1) @ w_out
    x = x + mod[2] * y
    x = x + mod[5] * _sqrelu_mlp(_modulate(_rmsnorm(x, g2), mod[3], mod[4]), w1, w2)

    if need_ctx:
        a_cc = _gated_head_norm(a_c, hgrn_g, jax.nn.silu(zc[2]))
        b_cc = _gated_head_norm(b_c, mlstm_g, jax.nn.sigmoid(zc[8]))
        c_cc = _merge(_dense_attn(qc_, kc_, vc_))
        d_cc = _merge(_dense_attn(cq, ck, cv))
        yc = jnp.concatenate([a_cc.astype(dt), b_cc.astype(dt), c_cc.astype(dt), d_cc.astype(dt)], axis=-1) @ w_out
        xc = xc + modc[2] * yc
        xc = xc + modc[5] * _sqrelu_mlp(_modulate(_rmsnorm(xc, g2), modc[3], modc[4]), w1, w2)
    return x, xc


def setup_inputs(seed: int = 0) -> dict:
    key = jax.random.key(seed)
    ks = jax.random.split(key, 20)
    d = D_MODEL
    nrm = lambda k, shape, s: jax.random.normal(k, shape, F32) * s
    gate_base = jnp.concatenate([jnp.zeros((2 * GROUP_HEADS,), F32),
                                 jnp.tile(jnp.linspace(3.0, 6.0, GROUP_HEADS, dtype=F32), 2)])
    return {
        "x": nrm(ks[0], (BATCH, SEQ, d), 1.0),
        "c": nrm(ks[1], (BATCH, d), 1.0),
        "ctx": nrm(ks[2], (BATCH, CTX_LEN, d), 1.0),
        "c_ctx": nrm(ks[3], (d,), 1.0),
        "w_mod": nrm(ks[4], (DEPTH, d, 6 * d), d ** -0.5),
        "b_mod": nrm(ks[5], (DEPTH, 6 * d), 0.02),
        "norm1_g": 1.0 + nrm(ks[6], (DEPTH, d), 0.05),
        "norm2_g": 1.0 + nrm(ks[7], (DEPTH, d), 0.05),
        "w_in": nrm(ks[8], (DEPTH, d, IN_WIDTH), d ** -0.5),
        "hgrn_lb_logits": nrm(ks[9], (DEPTH, 2, GROUP_WIDTH), 0.5),
        "hgrn_norm_g": 1.0 + nrm(ks[10], (DEPTH, HEAD_DIM), 0.05),
        "mlstm_gate_b": gate_base[None, :] + nrm(ks[11], (DEPTH, 4 * GROUP_HEADS), 0.1),
        "mlstm_norm_g": 1.0 + nrm(ks[12], (DEPTH, HEAD_DIM), 0.05),
        "gqa_qnorm_g": 1.0 + nrm(ks[13], (DEPTH, HEAD_DIM), 0.05),
        "gqa_knorm_g": 1.0 + nrm(ks[14], (DEPTH, HEAD_DIM), 0.05),
        "na_rpb": nrm(ks[15], (DEPTH, GROUP_HEADS, 2 * NA_KH - 1, 2 * NA_KW - 1), 0.1),
        "w_out": nrm(ks[16], (DEPTH, MIX_WIDTH, d), MIX_WIDTH ** -0.5),
        "w_mlp1": nrm(ks[17], (DEPTH, d, MLP_HIDDEN), d ** -0.5),
        "w_mlp2": nrm(ks[18], (DEPTH, MLP_HIDDEN, d), MLP_HIDDEN ** -0.5),
        "final_norm_g": 1.0 + nrm(ks[19], (d,), 0.05),
    }


def reference(x, c, ctx, c_ctx, w_mod, b_mod, norm1_g, norm2_g, w_in, hgrn_lb_logits, hgrn_norm_g,
              mlstm_gate_b, mlstm_norm_g, gqa_qnorm_g, gqa_knorm_g, na_rpb, w_out, w_mlp1, w_mlp2,
              final_norm_g):
    rope = _axial_rope(x.shape[1], x.dtype)
    sm = jax.nn.softmax(hgrn_lb_logits.astype(F32), axis=0)
    lbs = jnp.cumsum(sm, axis=0) - sm[0:1]
    xc = ctx
    for l in range(DEPTH):
        x, xc = _layer(x, xc, c, c_ctx, w_mod[l], b_mod[l], norm1_g[l], norm2_g[l], w_in[l], lbs[l],
                       hgrn_norm_g[l], mlstm_gate_b[l], mlstm_norm_g[l], gqa_qnorm_g[l], gqa_knorm_g[l],
                       na_rpb[l], w_out[l], w_mlp1[l], w_mlp2[l], rope, l < DEPTH - 1)
    return _rmsnorm(x, final_norm_g)
```

```python
import functools

import numpy as np
import jax
import jax.numpy as jnp
from jax import lax
from jax.experimental import pallas as pl
from jax.experimental.pallas import tpu as pltpu

F32 = jnp.float32
BF16 = jnp.bfloat16

D_MODEL = 1024
GRID_W = 64
CTX_LEN = 256
HEAD_DIM = 64
GROUP_HEADS = 4
GROUP_WIDTH = GROUP_HEADS * HEAD_DIM
KV_HEADS = 2
MLP_HIDDEN = 4 * D_MODEL
CHUNK = 64
SUB = 8
NA_KH = 8
NA_KW = 16
ROPE_THETA = 10000.0
EPS = 1e-6
NEG_BIG = -1e30
FORGET_FLOOR = 1e-20
ROW_TILE = 256
VMEM_LIMIT_BYTES = 56 * 1024 * 1024

A_OFF, A_W = 0, 5 * GROUP_WIDTH
B_OFF, B_W = A_OFF + A_W, 4 * GROUP_WIDTH
G_OFF, G_W = B_OFF + B_W, 128
C_OFF, C_W = G_OFF + G_W, 896
D_OFF, D_W = C_OFF + C_W, 3 * GROUP_WIDTH
IN_PACKED = D_OFF + D_W


def _dot(a, b):
    return jnp.dot(a, b, preferred_element_type=F32)


def _dot_nt(a, b):
    return lax.dot_general(a, b, (((1,), (1,)), ((), ())), preferred_element_type=F32)


def _dot_tn(a, b):
    return lax.dot_general(a, b, (((0,), (0,)), ((), ())), preferred_element_type=F32)


def _split3(x):
    hi = x.astype(BF16)
    r = x - hi.astype(F32)
    mid = r.astype(BF16)
    lo = (r - mid.astype(F32)).astype(BF16)
    return hi, mid, lo


def _route_l(m01, x):
    hi, mid, lo = _split3(x)
    return _dot(m01, hi) + _dot(m01, mid) + _dot(m01, lo)


def _route_r(x, m01):
    hi, mid, lo = _split3(x)
    return _dot(hi, m01) + _dot(mid, m01) + _dot(lo, m01)


def _sigmoid(x):
    return 1.0 / (1.0 + jnp.exp(-x))


def _log_sigmoid(x):
    return -(jnp.maximum(-x, 0.0) + jnp.log1p(jnp.exp(-jnp.abs(x))))


def _iota(shape, dim):
    return lax.broadcasted_iota(jnp.int32, shape, dim)


def _head_block_ones(n, m, dtype):
    return jnp.where((_iota((n, m), 0) >> 6) == (_iota((n, m), 1) >> 6), 1.0, 0.0).astype(dtype)


def _cumsum_matrix(rev):
    r, c = _iota((CHUNK, CHUNK), 0), _iota((CHUNK, CHUNK), 1)
    keep = (c >= r) if rev else (c <= r)
    return jnp.where(keep, 1.0, 0.0).astype(BF16)


def _head_rms(o, gain, ones_bd):
    ssq = _route_r(o * o, ones_bd)
    return o * lax.rsqrt(ssq * (1.0 / HEAD_DIM) + EPS) * gain


def _mod_kernel(c_ref, w_ref, b_ref, o_ref):
    c = c_ref[...]
    s = c * _sigmoid(c)
    o_ref[0] = _dot(s.astype(BF16), w_ref[0].astype(BF16)) + b_ref[0]


def _modulation(cc, w_mod, b_mod):
    depth, d, n = w_mod.shape
    rows = cc.shape[0]
    tn = 1024
    return pl.pallas_call(
        _mod_kernel,
        grid=(depth, n // tn),
        in_specs=[
            pl.BlockSpec((rows, d), lambda l, j: (0, 0)),
            pl.BlockSpec((1, d, tn), lambda l, j: (l, 0, j)),
            pl.BlockSpec((1, 1, tn), lambda l, j: (l, 0, j)),
        ],
        out_specs=pl.BlockSpec((1, rows, tn), lambda l, j: (l, 0, j)),
        out_shape=jax.ShapeDtypeStruct((depth, rows, n), F32),
        compiler_params=pltpu.CompilerParams(
            dimension_semantics=("arbitrary", "arbitrary"), vmem_limit_bytes=VMEM_LIMIT_BYTES),
        name="adaln_modulation",
    )(cc, w_mod, b_mod.reshape(depth, 1, n))


def _inproj_kernel(x_ref, mlat_ref, mctx_ref, g1_ref, w_ref, cos_ref, sin_ref, gq_ref, gqs_ref,
                   gk_ref, gks_ref, za_ref, zb_ref, zg_ref, qc_ref, kc_ref, vc_ref,
                   qd_ref, kd_ref, vd_ref):
    j = pl.program_id(1)
    mod = jnp.where(j == 0, mctx_ref[0], mlat_ref[0])
    x = x_ref[0]
    ms = jnp.mean(x * x, axis=-1, keepdims=True)
    y = x * lax.rsqrt(ms + EPS) * g1_ref[...]
    h = (y * (1.0 + mod[1:2]) + mod[0:1]).astype(BF16)

    za_ref[0] = _dot(h, w_ref[:, A_OFF:A_OFF + A_W])
    zb_ref[0] = _dot(h, w_ref[:, B_OFF:B_OFF + B_W])
    zg_ref[0] = _dot(h, w_ref[:, G_OFF:G_OFF + G_W])

    zc = _dot(h, w_ref[:, C_OFF:C_OFF + C_W])
    ones4 = _head_block_ones(GROUP_WIDTH, GROUP_WIDTH, BF16)
    cos, sin = cos_ref[...], sin_ref[...]
    q, qs = zc[:, 0:256], zc[:, 256:512]
    rq = lax.rsqrt(_route_r(q * q, ones4) * (1.0 / HEAD_DIM) + EPS)
    qr = (rq * (q * gq_ref[...] * cos + qs * gqs_ref[...] * sin)) * (HEAD_DIM ** -0.5)
    k, ks = zc[:, 512:640], zc[:, 640:768]
    rk = lax.rsqrt(_route_r(k * k, ones4[0:128, 0:128]) * (1.0 / HEAD_DIM) + EPS)
    kr = rk * (k * gk_ref[...] * cos[:, 0:128] + ks * gks_ref[...] * sin[:, 0:128])
    vv = zc[:, 768:896]
    for hh in range(GROUP_HEADS):
        qc_ref[0, hh] = qr[:, hh * HEAD_DIM:(hh + 1) * HEAD_DIM]
    for hh in range(KV_HEADS):
        kc_ref[0, hh] = kr[:, hh * HEAD_DIM:(hh + 1) * HEAD_DIM]
        vc_ref[0, hh] = vv[:, hh * HEAD_DIM:(hh + 1) * HEAD_DIM]

    zd = _dot(h, w_ref[:, D_OFF:D_OFF + D_W])
    for hh in range(GROUP_HEADS):
        lo, hi = hh * HEAD_DIM, (hh + 1) * HEAD_DIM
        qd_ref[0, hh] = zd[:, lo:hi] * (HEAD_DIM ** -0.5)
        kd_ref[0, hh] = zd[:, 256 + lo:256 + hi]
        vd_ref[0, hh] = zd[:, 512 + lo:512 + hi]


def _inproj(xcat, mods_l, g1, w_packed, cos4, sin4, gq, gqs, gk, gks):
    b, t, d = xcat.shape
    nt = t // ROW_TILE
    ctx_row = mods_l.shape[0] - 1
    full2 = lambda bb, j: (0, 0)
    headed = lambda nh: pl.BlockSpec((1, nh, ROW_TILE, HEAD_DIM), lambda bb, j: (bb, 0, j, 0))
    flat = lambda w: pl.BlockSpec((1, ROW_TILE, w), lambda bb, j: (bb, j, 0))
    hshape = lambda nh: jax.ShapeDtypeStruct((b, nh, t, HEAD_DIM), F32)
    fshape = lambda w: jax.ShapeDtypeStruct((b, t, w), F32)
    return pl.pallas_call(
        _inproj_kernel,
        grid=(b, nt),
        in_specs=[
            pl.BlockSpec((1, ROW_TILE, d), lambda bb, j: (bb, j, 0)),
            pl.BlockSpec((1, 6, d), lambda bb, j: (bb, 0, 0)),
            pl.BlockSpec((1, 6, d), lambda bb, j: (ctx_row, 0, 0)),
            pl.BlockSpec((1, d), full2),
            pl.BlockSpec((d, IN_PACKED), full2),
            pl.BlockSpec((ROW_TILE, GROUP_WIDTH), lambda bb, j: (j, 0)),
            pl.BlockSpec((ROW_TILE, GROUP_WIDTH), lambda bb, j: (j, 0)),
            pl.BlockSpec((1, GROUP_WIDTH), full2),
            pl.BlockSpec((1, GROUP_WIDTH), full2),
            pl.BlockSpec((1, 128), full2),
            pl.BlockSpec((1, 128), full2),
        ],
        out_specs=[flat(A_W), flat(B_W), flat(G_W), headed(4), headed(2), headed(2),
                   headed(4), headed(4), headed(4)],
        out_shape=[fshape(A_W), fshape(B_W), fshape(G_W), hshape(4), hshape(2), hshape(2),
                   hshape(4), hshape(4), hshape(4)],
        compiler_params=pltpu.CompilerParams(
            dimension_semantics=("parallel", "arbitrary"), vmem_limit_bytes=VMEM_LIMIT_BYTES),
        name="in_projection",
    )(xcat, mods_l, mods_l, g1, w_packed, cos4, sin4, gq, gqs, gk, gks)


def _hgrn_chunk(za_ref, o_ref, st_ref, row0, lb, rev, want_out, consts):
    cmat, ones_bd, bmask, hmasks = consts
    fcol = 4 * GROUP_WIDTH if rev else 3 * GROUP_WIDTH
    zq = za_ref[0, pl.ds(row0, CHUNK), 0:GROUP_WIDTH]
    v = za_ref[0, pl.ds(row0, CHUNK), GROUP_WIDTH:2 * GROUP_WIDTH]
    zf = za_ref[0, pl.ds(row0, CHUNK), fcol:fcol + GROUP_WIDTH]
    forget = lb + (1.0 - lb) * _sigmoid(zf)
    g = jnp.log(jnp.maximum(forget, FORGET_FLOOR))
    k = (1.0 - lb) * _sigmoid(-zf)
    b = _route_l(cmat, g)
    b_end = b[0:1] if rev else b[CHUNK - 1:CHUNK]
    st = st_ref[...]
    vb = v.astype(BF16)

    if want_out:
        q = zq * _sigmoid(zq) * (HEAD_DIM ** -0.5)
        o_inter = _dot_nt((q * jnp.exp(b)).astype(BF16), st.astype(BF16))
        rows_c = _iota((CHUNK, GROUP_WIDTH), 0)
        rows_s = _iota((SUB, GROUP_WIDTH), 0)
        outs = []
        nsub = CHUNK // SUB
        for i in range(nsub):
            lo = i * SUB
            bi, qi, ki, vi = b[lo:lo + SUB], q[lo:lo + SUB], k[lo:lo + SUB], v[lo:lo + SUB]
            ds = []
            for s in range(SUB):
                keep = (rows_s <= s) if rev else (rows_s >= s)
                e = jnp.exp(jnp.where(keep, bi - bi[s:s + 1], NEG_BIG))
                ds.append(qi * ki[s:s + 1] * e)
            abc = _dot(jnp.concatenate(ds, axis=0).astype(BF16), ones_bd)
            acc = o_inter[lo:lo + SUB]
            for s in range(SUB):
                acc = acc + abc[s * SUB:(s + 1) * SUB] * vi[s:s + 1]
            has_off = (i < nsub - 1) if rev else (i > 0)
            if has_off:
                m = b[lo + SUB:lo + SUB + 1] if rev else b[lo - 1:lo]
                valid = (rows_c >= lo + SUB) if rev else (rows_c < lo)
                qt = qi * jnp.exp(bi - m)
                kt = k * jnp.exp(jnp.where(valid, m - b, NEG_BIG))
                qs = jnp.concatenate([qt * hm for hm in hmasks], axis=0)
                a = _dot_nt(qs.astype(BF16), kt.astype(BF16))
                oo = _dot(a.astype(BF16), vb)
                for hh in range(GROUP_HEADS):
                    acc = acc + oo[hh * SUB:(hh + 1) * SUB] * hmasks[hh]
            outs.append(acc)
        o_ref[0, pl.ds(row0, CHUNK), :] += jnp.concatenate(outs, axis=0)

    kh = k * jnp.exp(b_end - b)
    st_ref[...] = st * jnp.exp(b_end) + _dot_tn(vb, kh.astype(BF16)) * bmask


def _hgrn_kernel(za_ref, lg_ref, gain_ref, o_ref, stf_ref, stb_ref, *, layer, need_ctx):
    t = za_ref.shape[1]
    n_ctx, n_all = CTX_LEN // CHUNK, t // CHUNK
    lg = lg_ref[...]
    depth = lg.shape[0]
    mx = lg[0]
    for i in range(1, depth):
        mx = jnp.maximum(mx, lg[i])
    es = [jnp.exp(lg[i] - mx) for i in range(depth)]
    tot = es[0]
    for i in range(1, depth):
        tot = tot + es[i]
    sm = [e / tot for e in es]
    cs = sm[0]
    for i in range(1, layer + 1):
        cs = cs + sm[i]
    lbs = cs - sm[0]
    lb_f, lb_b = lbs[0:1], lbs[1:2]

    ones_bd = _head_block_ones(GROUP_WIDTH, GROUP_WIDTH, BF16)
    bmask = _head_block_ones(GROUP_WIDTH, GROUP_WIDTH, F32)
    lane_head = _iota((1, GROUP_WIDTH), 1) >> 6
    hmasks = [jnp.where(lane_head == hh, 1.0, 0.0).astype(F32) for hh in range(GROUP_HEADS)]
    cf = (_cumsum_matrix(False), ones_bd, bmask, hmasks)
    cb = (_cumsum_matrix(True), ones_bd, bmask, hmasks)

    stf_ref[...] = jnp.zeros_like(stf_ref)
    stb_ref[...] = jnp.zeros_like(stb_ref)
    o_ref[...] = jnp.zeros_like(o_ref)

    def ctx_body(i, carry):
        _hgrn_chunk(za_ref, o_ref, stf_ref, pl.multiple_of(i * CHUNK, CHUNK), lb_f, False, need_ctx, cf)
        _hgrn_chunk(za_ref, o_ref, stb_ref, pl.multiple_of((n_ctx - 1 - i) * CHUNK, CHUNK), lb_b, True,
                    need_ctx, cb)
        return carry

    def lat_body(i, carry):
        _hgrn_chunk(za_ref, o_ref, stf_ref, pl.multiple_of((n_ctx + i) * CHUNK, CHUNK), lb_f, False, True, cf)
        _hgrn_chunk(za_ref, o_ref, stb_ref, pl.multiple_of((n_all - 1 - i) * CHUNK, CHUNK), lb_b, True,
                    True, cb)
        return carry

    lax.fori_loop(0, n_ctx, ctx_body, 0)
    lax.fori_loop(0, n_all - n_ctx, lat_body, 0)

    gain = gain_ref[...]

    def fin_body(i, carry):
        r0 = pl.multiple_of(i * ROW_TILE, ROW_TILE)
        o = o_ref[0, pl.ds(r0, ROW_TILE), :]
        zo = za_ref[0, pl.ds(r0, ROW_TILE), 2 * GROUP_WIDTH:3 * GROUP_WIDTH]
        o_ref[0, pl.ds(r0, ROW_TILE), :] = _head_rms(o, gain, ones_bd) * (zo * _sigmoid(zo))
        return carry

    lax.fori_loop(0, t // ROW_TILE, fin_body, 0)


def _hgrn(za, lb_logits, gain4, layer, need_ctx):
    b, t, _ = za.shape
    depth = lb_logits.shape[0]
    return pl.pallas_call(
        functools.partial(_hgrn_kernel, layer=layer, need_ctx=need_ctx),
        grid=(b,),
        in_specs=[
            pl.BlockSpec((1, t, A_W), lambda bb: (bb, 0, 0)),
            pl.BlockSpec((depth, 2, GROUP_WIDTH), lambda bb: (0, 0, 0)),
            pl.BlockSpec((1, GROUP_WIDTH), lambda bb: (0, 0)),
        ],
        out_specs=pl.BlockSpec((1, t, GROUP_WIDTH), lambda bb: (bb, 0, 0)),
        out_shape=jax.ShapeDtypeStruct((b, t, GROUP_WIDTH), F32),
        scratch_shapes=[pltpu.VMEM((GROUP_WIDTH, GROUP_WIDTH), F32),
                        pltpu.VMEM((GROUP_WIDTH, GROUP_WIDTH), F32)],
        compiler_params=pltpu.CompilerParams(
            dimension_semantics=("parallel",), vmem_limit_bytes=VMEM_LIMIT_BYTES),
        name="hgrn2_scan",
    )(za, lb_logits, gain4)


def _mlstm_chunk(zb_ref, zg_ref, o_ref, ct_ref, n_ref, m_ref, row0, gate_b, rev, want_out, consts):
    cmat, ones_bd, bmask, e_i, e_f, diag, ones_c, causal = consts
    q = zb_ref[0, pl.ds(row0, CHUNK), 0:GROUP_WIDTH]
    k = zb_ref[0, pl.ds(row0, CHUNK), GROUP_WIDTH:2 * GROUP_WIDTH] * (HEAD_DIM ** -0.5)
    v = zb_ref[0, pl.ds(row0, CHUNK), 2 * GROUP_WIDTH:3 * GROUP_WIDTH]
    zg = zg_ref[0, pl.ds(row0, CHUNK), :] + gate_b
    ig = _route_r(zg, e_i)
    lf = _log_sigmoid(_route_r(zg, e_f))
    bc = _route_l(cmat, lf)
    b_end = bc[0:1] if rev else bc[CHUNK - 1:CHUNK]
    m_prev = m_ref[0:1]
    n_prev = n_ref[0:1]
    ct = ct_ref[...]
    vb = v.astype(BF16)

    if want_out:
        qb = q.astype(BF16)
        rowv = _route_l(ones_c, (ig - bc) * diag)
        dlog = jnp.where(causal, bc + rowv, NEG_BIG)
        segs = []
        for hh in range(GROUP_HEADS):
            mh = jnp.max(dlog[:, hh * HEAD_DIM:(hh + 1) * HEAD_DIM], axis=-1, keepdims=True)
            segs.append(jnp.broadcast_to(mh, (CHUNK, HEAD_DIM)))
        a = bc + m_prev
        m_t = jnp.maximum(a, jnp.concatenate(segs, axis=-1))
        w_in = jnp.exp(a - m_t)
        kk = jnp.concatenate([k] * GROUP_HEADS, axis=0) * bmask
        vv = jnp.concatenate([v] * GROUP_HEADS, axis=0) * bmask
        p = jnp.exp(dlog - m_t) * _dot_nt(qb, kk.astype(BF16))
        pb = p.astype(BF16)
        num = w_in * _dot_nt(qb, ct.astype(BF16)) + _dot(pb, vv.astype(BF16))
        qn = _dot_nt(qb, (n_prev * bmask).astype(BF16))
        den = w_in * qn + _dot(pb, ones_bd)
        o_ref[0, pl.ds(row0, CHUNK), :] += num / jnp.maximum(jnp.abs(den), jnp.exp(-m_t))

    gs = b_end - bc + ig
    m_new = jnp.maximum(b_end + m_prev, jnp.max(gs, axis=0, keepdims=True))
    w_old = jnp.exp(b_end + m_prev - m_new)
    kw = k * jnp.exp(gs - m_new)
    ct_ref[...] = ct * w_old + _dot_tn(vb, kw.astype(BF16)) * bmask
    n_ref[...] = jnp.broadcast_to(n_prev * w_old + jnp.sum(kw, axis=0, keepdims=True), n_ref.shape)
    m_ref[...] = jnp.broadcast_to(m_new, m_ref.shape)


def _mlstm_kernel(zb_ref, zg_ref, gb_ref, gain_ref, o_ref, ctf_ref, ctb_ref, nf_ref, nb_ref,
                  mf_ref, mb_ref, *, need_ctx):
    t = zb_ref.shape[1]
    n_ctx, n_all = CTX_LEN // CHUNK, t // CHUNK
    ones_bd = _head_block_ones(GROUP_WIDTH, GROUP_WIDTH, BF16)
    bmask = _head_block_ones(GROUP_WIDTH, GROUP_WIDTH, F32)
    gcol = _iota((128, GROUP_WIDTH), 0)
    ghead = _iota((128, GROUP_WIDTH), 1) >> 6
    route = lambda base: jnp.where(gcol == base + ghead, 1.0, 0.0).astype(BF16)
    tt = _iota((CHUNK, GROUP_WIDTH), 0)
    ss = _iota((CHUNK, GROUP_WIDTH), 1) & (HEAD_DIM - 1)
    diag = jnp.where(tt == ss, 1.0, 0.0).astype(F32)
    ones_c = jnp.ones((CHUNK, CHUNK), BF16)
    cf = (_cumsum_matrix(False), ones_bd, bmask, route(0), route(2 * GROUP_HEADS), diag, ones_c, ss <= tt)
    cb = (_cumsum_matrix(True), ones_bd, bmask, route(GROUP_HEADS), route(3 * GROUP_HEADS), diag, ones_c,
          ss >= tt)
    gate_b = gb_ref[...]

    for r in (ctf_ref, ctb_ref, nf_ref, nb_ref, mf_ref, mb_ref, o_ref):
        r[...] = jnp.zeros_like(r)

    def step(cf_row, cb_row, want):
        _mlstm_chunk(zb_ref, zg_ref, o_ref, ctf_ref, nf_ref, mf_ref, cf_row, gate_b, False, want, cf)
        _mlstm_chunk(zb_ref, zg_ref, o_ref, ctb_ref, nb_ref, mb_ref, cb_row, gate_b, True, want, cb)

    def ctx_body(i, carry):
        step(pl.multiple_of(i * CHUNK, CHUNK), pl.multiple_of((n_ctx - 1 - i) * CHUNK, CHUNK), need_ctx)
        return carry

    def lat_body(i, carry):
        step(pl.multiple_of((n_ctx + i) * CHUNK, CHUNK), pl.multiple_of((n_all - 1 - i) * CHUNK, CHUNK), True)
        return carry

    lax.fori_loop(0, n_ctx, ctx_body, 0)
    lax.fori_loop(0, n_all - n_ctx, lat_body, 0)

    gain = gain_ref[...]

    def fin_body(i, carry):
        r0 = pl.multiple_of(i * ROW_TILE, ROW_TILE)
        o = o_ref[0, pl.ds(r0, ROW_TILE), :]
        zo = zb_ref[0, pl.ds(r0, ROW_TILE), 3 * GROUP_WIDTH:4 * GROUP_WIDTH]
        o_ref[0, pl.ds(r0, ROW_TILE), :] = _head_rms(o, gain, ones_bd) * _sigmoid(zo)
        return carry

    lax.fori_loop(0, t // ROW_TILE, fin_body, 0)


def _mlstm(zb, zg, gate_b128, gain4, need_ctx):
    b, t, _ = zb.shape
    return pl.pallas_call(
        functools.partial(_mlstm_kernel, need_ctx=need_ctx),
        grid=(b,),
        in_specs=[
            pl.BlockSpec((1, t, B_W), lambda bb: (bb, 0, 0)),
            pl.BlockSpec((1, t, G_W), lambda bb: (bb, 0, 0)),
            pl.BlockSpec((1, G_W), lambda bb: (0, 0)),
            pl.BlockSpec((1, GROUP_WIDTH), lambda bb: (0, 0)),
        ],
        out_specs=pl.BlockSpec((1, t, GROUP_WIDTH), lambda bb: (bb, 0, 0)),
        out_shape=jax.ShapeDtypeStruct((b, t, GROUP_WIDTH), F32),
        scratch_shapes=[pltpu.VMEM((GROUP_WIDTH, GROUP_WIDTH), F32),
                        pltpu.VMEM((GROUP_WIDTH, GROUP_WIDTH), F32),
                        pltpu.VMEM((SUB, GROUP_WIDTH), F32), pltpu.VMEM((SUB, GROUP_WIDTH), F32),
                        pltpu.VMEM((SUB, GROUP_WIDTH), F32), pltpu.VMEM((SUB, GROUP_WIDTH), F32)],
        compiler_params=pltpu.CompilerParams(
            dimension_semantics=("parallel",), vmem_limit_bytes=VMEM_LIMIT_BYTES),
        name="mlstm_scan",
    )(zb, zg, gate_b128, gain4)


def _softmax_pv(s_list, v_list):
    m = s_list[0].max(axis=-1, keepdims=True)
    for s in s_list[1:]:
        m = jnp.maximum(m, s.max(axis=-1, keepdims=True))
    l = None
    acc = None
    for s, v in zip(s_list, v_list):
        p = jnp.exp(s - m)
        ls = jnp.sum(p, axis=-1, keepdims=True)
        pv = _dot(p.astype(BF16), v)
        l = ls if l is None else l + ls
        acc = pv if acc is None else acc + pv
    return acc / l


def _gqa_kernel(q_ref, k_ref, v_ref, o_ref, *, need_ctx):
    j = pl.program_id(2)
    t = k_ref.shape[2]

    def attend(nk):
        k = k_ref[0, 0, 0:nk, :].astype(BF16)
        v = v_ref[0, 0, 0:nk, :].astype(BF16)
        outs = []
        for g in range(GROUP_HEADS // KV_HEADS):
            s = _dot_nt(q_ref[0, g].astype(BF16), k)
            outs.append(_softmax_pv([s], [v]))
        o_ref[0] = jnp.concatenate(outs, axis=-1)

    @pl.when(j == 0)
    def _():
        if need_ctx:
            attend(CTX_LEN)
        else:
            o_ref[...] = jnp.zeros_like(o_ref)

    @pl.when(j > 0)
    def _():
        attend(t)


def _gqa(qc, kc, vc, need_ctx):
    b, _, t, _ = qc.shape
    nq = t // ROW_TILE
    g = GROUP_HEADS // KV_HEADS
    return pl.pallas_call(
        functools.partial(_gqa_kernel, need_ctx=need_ctx),
        grid=(b, KV_HEADS, nq),
        in_specs=[
            pl.BlockSpec((1, g, ROW_TILE, HEAD_DIM), lambda bb, h, j: (bb, h, j, 0)),
            pl.BlockSpec((1, 1, t, HEAD_DIM), lambda bb, h, j: (bb, h, 0, 0)),
            pl.BlockSpec((1, 1, t, HEAD_DIM), lambda bb, h, j: (bb, h, 0, 0)),
        ],
        out_specs=pl.BlockSpec((1, ROW_TILE, g * HEAD_DIM), lambda bb, h, j: (bb, j, h)),
        out_shape=jax.ShapeDtypeStruct((b, t, GROUP_WIDTH), F32),
        compiler_params=pltpu.CompilerParams(
            dimension_semantics=("parallel", "parallel", "arbitrary"), vmem_limit_bytes=VMEM_LIMIT_BYTES),
        name="gqa_attention",
    )(qc, kc, vc)


def _na_kernel(q_ref, k_ref, v_ref, bias_ref, o_ref, *, need_ctx):
    t = q_ref.shape[2]
    rows = (t - CTX_LEN) // GRID_W
    kh = min(NA_KH, rows)
    nloc = kh * GRID_W
    heads = q_ref.shape[1]

    for g in range(heads):
        lane = slice(g * HEAD_DIM, (g + 1) * HEAD_DIM)
        kctx = k_ref[0, g, 0:CTX_LEN, :].astype(BF16)
        vctx = v_ref[0, g, 0:CTX_LEN, :].astype(BF16)
        if need_ctx:
            s = _dot_nt(q_ref[0, g, 0:CTX_LEN, :].astype(BF16), kctx)
            o_ref[0, 0:CTX_LEN, lane] = _softmax_pv([s], [vctx])
        else:
            o_ref[0, 0:CTX_LEN, lane] = jnp.zeros((CTX_LEN, HEAD_DIM), F32)

        def row_body(r, carry):
            r0 = jnp.clip(r - kh // 2, 0, rows - kh)
            qrow = pl.multiple_of(CTX_LEN + r * GRID_W, GRID_W)
            krow = pl.multiple_of(CTX_LEN + r0 * GRID_W, GRID_W)
            q = q_ref[0, g, pl.ds(qrow, GRID_W), :].astype(BF16)
            kl = k_ref[0, g, pl.ds(krow, nloc), :].astype(BF16)
            vl = v_ref[0, g, pl.ds(krow, nloc), :].astype(BF16)
            s_loc = _dot_nt(q, kl) + bias_ref[g, r - r0]
            s_ctx = _dot_nt(q, kctx)
            o_ref[0, pl.ds(qrow, GRID_W), lane] = _softmax_pv([s_ctx, s_loc], [vctx, vl])
            return carry

        lax.fori_loop(0, rows, row_body, 0)


def _na(qd, kd, vd, bias, need_ctx):
    b, _, t, _ = qd.shape
    hp = 2
    nvar, wq, nloc = bias.shape[1:]
    blk = pl.BlockSpec((1, hp, t, HEAD_DIM), lambda bb, h: (bb, h, 0, 0))
    return pl.pallas_call(
        functools.partial(_na_kernel, need_ctx=need_ctx),
        grid=(b, GROUP_HEADS // hp),
        in_specs=[blk, blk, blk,
                  pl.BlockSpec((hp, nvar, wq, nloc), lambda bb, h: (h, 0, 0, 0))],
        out_specs=pl.BlockSpec((1, t, hp * HEAD_DIM), lambda bb, h: (bb, 0, h)),
        out_shape=jax.ShapeDtypeStruct((b, t, GROUP_WIDTH), F32),
        compiler_params=pltpu.CompilerParams(
            dimension_semantics=("parallel", "arbitrary"), vmem_limit_bytes=VMEM_LIMIT_BYTES),
        name="neighbourhood_attention",
    )(qd, kd, vd, bias)


def _na_bias_table(rpb, rows):
    kh = min(NA_KH, rows)
    col = np.arange(GRID_W)
    cstart = np.clip(col - NA_KW // 2, 0, GRID_W - NA_KW)
    col_in = (col[None, :] >= cstart[:, None]) & (col[None, :] < cstart[:, None] + NA_KW)
    col_idx = np.clip(col[None, :] - col[:, None], 1 - NA_KW, NA_KW - 1) + NA_KW - 1
    var = np.arange(kh)
    row_idx = np.arange(kh)[None, :] - var[:, None] + NA_KH - 1
    valid = (row_idx >= 0) & (row_idx < 2 * NA_KH - 1)
    row_idx = np.clip(row_idx, 0, 2 * NA_KH - 2)
    tbl = rpb[:, row_idx][:, :, :, col_idx]
    keep = col_in[None, None, None] & valid[None, :, :, None, None]
    tbl = jnp.where(keep, tbl.astype(F32), NEG_BIG)
    tbl = tbl.transpose(0, 1, 3, 2, 4)
    return tbl.reshape(rpb.shape[0], kh, GRID_W, kh * GRID_W)


def _outmlp_kernel(x_ref, a_ref, b_ref, c_ref, d_ref, mlat_ref, mctx_ref, g2_ref, wo_ref, w1_ref,
                   w2_ref, gf_ref, o_ref, *, tile_off, final):
    j = pl.program_id(1) + tile_off
    mod = jnp.where(j == 0, mctx_ref[0], mlat_ref[0])
    cat = jnp.concatenate([a_ref[0], b_ref[0], c_ref[0], d_ref[0]], axis=-1).astype(BF16)
    x1 = x_ref[0] + mod[2:3] * _dot(cat, wo_ref[...])
    ms = jnp.mean(x1 * x1, axis=-1, keepdims=True)
    h2 = ((x1 * lax.rsqrt(ms + EPS) * g2_ref[...]) * (1.0 + mod[4:5]) + mod[3:4]).astype(BF16)
    hid = w1_ref.shape[1]
    step = 1024
    acc = jnp.zeros(x1.shape, F32)
    for c in range(hid // step):
        u = jnp.maximum(_dot(h2, w1_ref[:, c * step:(c + 1) * step]), 0.0)
        acc = acc + _dot((u * u).astype(BF16), w2_ref[c * step:(c + 1) * step, :])
    x2 = x1 + mod[5:6] * acc
    if final:
        ms2 = jnp.mean(x2 * x2, axis=-1, keepdims=True)
        x2 = x2 * lax.rsqrt(ms2 + EPS) * gf_ref[...]
    o_ref[0] = x2


def _outmlp(xcat, a, bm, c, dd, mods_l, g2, wo, w1, w2, gf, final):
    b, t, d = xcat.shape
    tile_off = CTX_LEN // ROW_TILE if final else 0
    nt = t // ROW_TILE - tile_off
    ctx_row = mods_l.shape[0] - 1
    full2 = lambda bb, j: (0, 0)
    grp = pl.BlockSpec((1, ROW_TILE, GROUP_WIDTH), lambda bb, j: (bb, j + tile_off, 0))
    return pl.pallas_call(
        functools.partial(_outmlp_kernel, tile_off=tile_off, final=final),
        grid=(b, nt),
        in_specs=[
            pl.BlockSpec((1, ROW_TILE, d), lambda bb, j: (bb, j + tile_off, 0)),
            grp, grp, grp, grp,
            pl.BlockSpec((1, 6, d), lambda bb, j: (bb, 0, 0)),
            pl.BlockSpec((1, 6, d), lambda bb, j: (ctx_row, 0, 0)),
            pl.BlockSpec((1, d), full2),
            pl.BlockSpec(wo.shape, full2),
            pl.BlockSpec(w1.shape, full2),
            pl.BlockSpec(w2.shape, full2),
            pl.BlockSpec((1, d), full2),
        ],
        out_specs=pl.BlockSpec((1, ROW_TILE, d), lambda bb, j: (bb, j, 0)),
        out_shape=jax.ShapeDtypeStruct((b, nt * ROW_TILE, d), F32),
        compiler_params=pltpu.CompilerParams(
            dimension_semantics=("parallel", "arbitrary"), vmem_limit_bytes=VMEM_LIMIT_BYTES),
        name="out_projection_mlp",
    )(xcat, a, bm, c, dd, mods_l, mods_l, g2, wo, w1, w2, gf)


def _packed_columns():
    gw = GROUP_WIDTH
    a0 = 0
    b0 = a0 + 5 * gw
    g0 = b0 + 4 * gw
    c0 = g0 + 4 * GROUP_HEADS
    d0 = c0 + gw + 2 * KV_HEADS * HEAD_DIM
    swap = lambda n: np.arange(n) ^ 1
    cols = [np.arange(a0, a0 + 5 * gw), np.arange(b0, b0 + 4 * gw),
            np.arange(g0, g0 + 4 * GROUP_HEADS), np.full(G_W - 4 * GROUP_HEADS, -1),
            c0 + np.arange(gw), c0 + swap(gw),
            c0 + gw + np.arange(128), c0 + gw + swap(128),
            c0 + gw + 128 + np.arange(128),
            np.arange(d0, d0 + 3 * gw)]
    cols = np.concatenate(cols)
    assert cols.shape[0] == IN_PACKED
    return cols


def _rope_tables(n_tokens):
    t = jnp.arange(n_tokens)
    row = (t // GRID_W).astype(F32)
    col = (t % GRID_W).astype(F32)
    axis_dims = HEAD_DIM // 2
    inv = jnp.power(ROPE_THETA, -2.0 * jnp.arange(axis_dims // 2, dtype=F32) / axis_dims)
    ang = jnp.concatenate([row[:, None] * inv, col[:, None] * inv], axis=-1)
    cos = jnp.repeat(jnp.cos(ang).astype(F32), 2, axis=-1)
    sign = jnp.where(jnp.arange(HEAD_DIM) % 2 == 0, -1.0, 1.0).astype(F32)
    sin = jnp.repeat(jnp.sin(ang).astype(F32), 2, axis=-1) * sign
    cos = jnp.concatenate([jnp.ones((CTX_LEN, HEAD_DIM), F32), cos], axis=0)
    sin = jnp.concatenate([jnp.zeros((CTX_LEN, HEAD_DIM), F32), sin], axis=0)
    return jnp.tile(cos, (1, GROUP_HEADS)), jnp.tile(sin, (1, GROUP_HEADS))


def kernel(x, c, ctx, c_ctx, w_mod, b_mod, norm1_g, norm2_g, w_in, hgrn_lb_logits, hgrn_norm_g,
           mlstm_gate_b, mlstm_norm_g, gqa_qnorm_g, gqa_knorm_g, na_rpb, w_out, w_mlp1, w_mlp2,
           final_norm_g):
    bsz, seq, d = x.shape
    depth = w_mod.shape[0]
    rows = seq // GRID_W

    n_c = bsz + 1
    pad = (-n_c) % 8
    cc = jnp.concatenate([c, c_ctx[None, :], jnp.zeros((pad, d), F32)], axis=0)
    mods = _modulation(cc, w_mod, b_mod)[:, :n_c].reshape(depth, n_c, 6, d)

    cols = _packed_columns()
    colmask = jnp.asarray(cols >= 0)
    cos4, sin4 = _rope_tables(seq)
    swap = np.arange(HEAD_DIM) ^ 1

    xcat = jnp.concatenate([ctx, x], axis=1)
    out = None
    for l in range(depth):
        need_ctx = l < depth - 1
        w_packed = jnp.where(colmask[None, :], w_in[l][:, np.maximum(cols, 0)], 0.0).astype(BF16)
        gq = jnp.tile(gqa_qnorm_g[l], GROUP_HEADS)[None, :]
        gqs = jnp.tile(gqa_qnorm_g[l][swap], GROUP_HEADS)[None, :]
        gk = jnp.tile(gqa_knorm_g[l], KV_HEADS)[None, :]
        gks = jnp.tile(gqa_knorm_g[l][swap], KV_HEADS)[None, :]
        za, zb, zg, qc, kc, vc, qd, kd, vd = _inproj(
            xcat, mods[l], norm1_g[l][None, :], w_packed, cos4, sin4, gq, gqs, gk, gks)

        a = _hgrn(za, hgrn_lb_logits, jnp.tile(hgrn_norm_g[l], GROUP_HEADS)[None, :], l, need_ctx)
        gate_b = jnp.concatenate([mlstm_gate_b[l], jnp.zeros((G_W - 4 * GROUP_HEADS,), F32)])[None, :]
        bm = _mlstm(zb, zg, gate_b, jnp.tile(mlstm_norm_g[l], GROUP_HEADS)[None, :], need_ctx)
        cg = _gqa(qc, kc, vc, need_ctx)
        dg = _na(qd, kd, vd, _na_bias_table(na_rpb[l], rows), need_ctx)

        res = _outmlp(xcat, a, bm, cg, dg, mods[l], norm2_g[l][None, :], w_out[l].astype(BF16),
                      w_mlp1[l].astype(BF16), w_mlp2[l].astype(BF16), final_norm_g[None, :],
                      final=not need_ctx)
        if need_ctx:
            xcat = res
        else:
            out = res
    return out
```

```python
import functools

import numpy as np
import jax
import jax.numpy as jnp
from jax import lax
from jax.experimental import pallas as pl
from jax.experimental.pallas import tpu as pltpu

F32 = jnp.float32
BF16 = jnp.bfloat16

D_MODEL = 1024
GRID_W = 64
CTX_LEN = 256
HEAD_DIM = 64
GROUP_HEADS = 4
GROUP_WIDTH = GROUP_HEADS * HEAD_DIM
KV_HEADS = 2
MLP_HIDDEN = 4 * D_MODEL
CHUNK = 64
SUB = 8
NA_KH = 8
NA_KW = 16
ROPE_THETA = 10000.0
EPS = 1e-6
NEG_BIG = -1e30
FORGET_FLOOR = 1e-20
ROW_TILE = 256
VMEM_LIMIT_BYTES = 56 * 1024 * 1024
LOG2E = 1.4426950408889634
Q_SCALE = HEAD_DIM ** -0.5 * LOG2E
NA_QROWS = 4

A_OFF, A_W = 0, 5 * GROUP_WIDTH
B_OFF, B_W = A_OFF + A_W, 4 * GROUP_WIDTH
G_OFF, G_W = B_OFF + B_W, 128
C_OFF, C_W = G_OFF + G_W, 896
D_OFF, D_W = C_OFF + C_W, 3 * GROUP_WIDTH
IN_PACKED = D_OFF + D_W


def _dot(a, b):
    return jnp.dot(a, b, preferred_element_type=F32)


def _dot_nt(a, b):
    return lax.dot_general(a, b, (((1,), (1,)), ((), ())), preferred_element_type=F32)


def _dot_tn(a, b):
    return lax.dot_general(a, b, (((0,), (0,)), ((), ())), preferred_element_type=F32)


def _split3(x):
    hi = x.astype(BF16)
    r = x - hi.astype(F32)
    mid = r.astype(BF16)
    lo = (r - mid.astype(F32)).astype(BF16)
    return hi, mid, lo


def _route_l(m01, x):
    hi, mid, lo = _split3(x)
    return _dot(m01, hi) + _dot(m01, mid) + _dot(m01, lo)


def _route_r(x, m01):
    hi, mid, lo = _split3(x)
    return _dot(hi, m01) + _dot(mid, m01) + _dot(lo, m01)


def _sigmoid(x):
    return 1.0 / (1.0 + jnp.exp(-x))


def _log_sigmoid(x):
    return -(jnp.maximum(-x, 0.0) + jnp.log1p(jnp.exp(-jnp.abs(x))))


def _iota(shape, dim):
    return lax.broadcasted_iota(jnp.int32, shape, dim)


def _head_block_ones(n, m, dtype):
    return jnp.where((_iota((n, m), 0) >> 6) == (_iota((n, m), 1) >> 6), 1.0, 0.0).astype(dtype)


def _cumsum_matrix(rev):
    r, c = _iota((CHUNK, CHUNK), 0), _iota((CHUNK, CHUNK), 1)
    keep = (c >= r) if rev else (c <= r)
    return jnp.where(keep, 1.0, 0.0).astype(BF16)


def _head_rms(o, gain, ones_bd):
    ssq = _route_r(o * o, ones_bd)
    return o * lax.rsqrt(ssq * (1.0 / HEAD_DIM) + EPS) * gain


def _mod_kernel(c_ref, w_ref, b_ref, o_ref):
    c = c_ref[...]
    s = c * _sigmoid(c)
    o_ref[0] = _dot(s.astype(BF16), w_ref[0].astype(BF16)) + b_ref[0]


def _modulation(cc, w_mod, b_mod):
    depth, d, n = w_mod.shape
    rows = cc.shape[0]
    tn = 1024
    return pl.pallas_call(
        _mod_kernel,
        grid=(depth, n // tn),
        in_specs=[
            pl.BlockSpec((rows, d), lambda l, j: (0, 0)),
            pl.BlockSpec((1, d, tn), lambda l, j: (l, 0, j)),
            pl.BlockSpec((1, 1, tn), lambda l, j: (l, 0, j)),
        ],
        out_specs=pl.BlockSpec((1, rows, tn), lambda l, j: (l, 0, j)),
        out_shape=jax.ShapeDtypeStruct((depth, rows, n), F32),
        compiler_params=pltpu.CompilerParams(
            dimension_semantics=("arbitrary", "arbitrary"), vmem_limit_bytes=VMEM_LIMIT_BYTES),
        name="adaln_modulation",
    )(cc, w_mod, b_mod.reshape(depth, 1, n))


def _inproj_kernel(x_ref, mlat_ref, mctx_ref, g1_ref, w_ref, cos_ref, sin_ref, gq_ref, gqs_ref,
                   gk_ref, gks_ref, za_ref, zb_ref, zg_ref, qc_ref, kc_ref, vc_ref,
                   qd_ref, kd_ref, vd_ref):
    j = pl.program_id(1)
    mod = jnp.where(j == 0, mctx_ref[0], mlat_ref[0])
    x = x_ref[0]
    ms = jnp.mean(x * x, axis=-1, keepdims=True)
    y = x * lax.rsqrt(ms + EPS) * g1_ref[...]
    h = (y * (1.0 + mod[1:2]) + mod[0:1]).astype(BF16)

    za_ref[0] = _dot(h, w_ref[:, A_OFF:A_OFF + A_W])
    zb_ref[0] = _dot(h, w_ref[:, B_OFF:B_OFF + B_W])
    zg_ref[0] = _dot(h, w_ref[:, G_OFF:G_OFF + G_W])

    zc = _dot(h, w_ref[:, C_OFF:C_OFF + C_W])
    ones4 = _head_block_ones(GROUP_WIDTH, GROUP_WIDTH, BF16)
    cos, sin = cos_ref[...], sin_ref[...]
    q, qs = zc[:, 0:256], zc[:, 256:512]
    rq = lax.rsqrt(_route_r(q * q, ones4) * (1.0 / HEAD_DIM) + EPS)
    qrt = ((rq * (q * gq_ref[...] * cos + qs * gqs_ref[...] * sin)) * Q_SCALE).T
    k, ks = zc[:, 512:640], zc[:, 640:768]
    rk = lax.rsqrt(_route_r(k * k, ones4[0:128, 0:128]) * (1.0 / HEAD_DIM) + EPS)
    kr = (rk * (k * gk_ref[...] * cos[:, 0:128] + ks * gks_ref[...] * sin[:, 0:128])).astype(BF16)
    vvt = zc[:, 768:896].T
    for hh in range(GROUP_HEADS):
        qc_ref[0, hh] = qrt[hh * HEAD_DIM:(hh + 1) * HEAD_DIM].astype(BF16)
    for hh in range(KV_HEADS):
        kc_ref[0, hh] = kr[:, hh * HEAD_DIM:(hh + 1) * HEAD_DIM]
        vc_ref[0, hh] = vvt[hh * HEAD_DIM:(hh + 1) * HEAD_DIM].astype(BF16)

    zd = _dot(h, w_ref[:, D_OFF:D_OFF + D_W])
    qdt = (zd[:, 0:256] * Q_SCALE).T
    kdb = zd[:, 256:512].astype(BF16)
    vdt = zd[:, 512:768].T
    for pp in range(GROUP_HEADS // 2):
        lo, hi = pp * 2 * HEAD_DIM, (pp + 1) * 2 * HEAD_DIM
        qd_ref[0, pp] = qdt[lo:hi].astype(BF16)
        kd_ref[0, pp] = kdb[:, lo:hi]
        vd_ref[0, pp] = vdt[lo:hi].astype(BF16)


def _inproj(xcat, mods_l, g1, w_packed, cos4, sin4, gq, gqs, gk, gks):
    b, t, d = xcat.shape
    nt = t // ROW_TILE
    ctx_row = mods_l.shape[0] - 1
    full2 = lambda bb, j: (0, 0)
    headed = lambda nh, w=HEAD_DIM: pl.BlockSpec((1, nh, ROW_TILE, w), lambda bb, j: (bb, 0, j, 0))
    headed_t = lambda nh, w=HEAD_DIM: pl.BlockSpec((1, nh, w, ROW_TILE), lambda bb, j: (bb, 0, 0, j))
    flat = lambda w: pl.BlockSpec((1, ROW_TILE, w), lambda bb, j: (bb, j, 0))
    hshape = lambda nh, w=HEAD_DIM: jax.ShapeDtypeStruct((b, nh, t, w), BF16)
    hshape_t = lambda nh, w=HEAD_DIM: jax.ShapeDtypeStruct((b, nh, w, t), BF16)
    fshape = lambda w: jax.ShapeDtypeStruct((b, t, w), F32)
    pw = 2 * HEAD_DIM
    return pl.pallas_call(
        _inproj_kernel,
        grid=(b, nt),
        in_specs=[
            pl.BlockSpec((1, ROW_TILE, d), lambda bb, j: (bb, j, 0)),
            pl.BlockSpec((1, 6, d), lambda bb, j: (bb, 0, 0)),
            pl.BlockSpec((1, 6, d), lambda bb, j: (ctx_row, 0, 0)),
            pl.BlockSpec((1, d), full2),
            pl.BlockSpec((d, IN_PACKED), full2),
            pl.BlockSpec((ROW_TILE, GROUP_WIDTH), lambda bb, j: (j, 0)),
            pl.BlockSpec((ROW_TILE, GROUP_WIDTH), lambda bb, j: (j, 0)),
            pl.BlockSpec((1, GROUP_WIDTH), full2),
            pl.BlockSpec((1, GROUP_WIDTH), full2),
            pl.BlockSpec((1, 128), full2),
            pl.BlockSpec((1, 128), full2),
        ],
        out_specs=[flat(A_W), flat(B_W), flat(G_W), headed_t(4), headed(2), headed_t(2),
                   headed_t(2, pw), headed(2, pw), headed_t(2, pw)],
        out_shape=[fshape(A_W), fshape(B_W), fshape(G_W), hshape_t(4), hshape(2), hshape_t(2),
                   hshape_t(2, pw), hshape(2, pw), hshape_t(2, pw)],
        compiler_params=pltpu.CompilerParams(
            dimension_semantics=("parallel", "arbitrary"), vmem_limit_bytes=VMEM_LIMIT_BYTES),
        name="in_projection",
    )(xcat, mods_l, mods_l, g1, w_packed, cos4, sin4, gq, gqs, gk, gks)


def _hgrn_tile_gates(za_ref, row0, lb, rev, cmat4):
    fcol = 4 * GROUP_WIDTH if rev else 3 * GROUP_WIDTH
    zf = za_ref[0, pl.ds(row0, ROW_TILE), fcol:fcol + GROUP_WIDTH]
    forget = lb + (1.0 - lb) * _sigmoid(zf)
    g = jnp.log(jnp.maximum(forget, FORGET_FLOOR))
    k = (1.0 - lb) * _sigmoid(-zf)
    return k, _route_l(cmat4, g)


def _hgrn_chunk(za_ref, o_ref, st_ref, row0, k, b, rev, want_out, consts):
    ones_bd, bmask, hmasks = consts
    zq = za_ref[0, pl.ds(row0, CHUNK), 0:GROUP_WIDTH]
    v = za_ref[0, pl.ds(row0, CHUNK), GROUP_WIDTH:2 * GROUP_WIDTH]
    b_end = b[0:1] if rev else b[CHUNK - 1:CHUNK]
    st = st_ref[...]
    vb = v.astype(BF16)

    if want_out:
        q = zq * _sigmoid(zq) * (HEAD_DIM ** -0.5)
        o_inter = _dot_nt((q * jnp.exp(b)).astype(BF16), st.astype(BF16))
        rows_c = _iota((CHUNK, GROUP_WIDTH), 0)
        rows_s = _iota((SUB, GROUP_WIDTH), 0)
        outs = []
        nsub = CHUNK // SUB
        for i in range(nsub):
            lo = i * SUB
            bi, qi, ki, vi = b[lo:lo + SUB], q[lo:lo + SUB], k[lo:lo + SUB], v[lo:lo + SUB]
            ds = []
            for s in range(SUB):
                keep = (rows_s <= s) if rev else (rows_s >= s)
                e = jnp.exp(jnp.where(keep, bi - bi[s:s + 1], NEG_BIG))
                ds.append(qi * ki[s:s + 1] * e)
            abc = _dot(jnp.concatenate(ds, axis=0).astype(BF16), ones_bd)
            acc = o_inter[lo:lo + SUB]
            for s in range(SUB):
                acc = acc + abc[s * SUB:(s + 1) * SUB] * vi[s:s + 1]
            has_off = (i < nsub - 1) if rev else (i > 0)
            if has_off:
                m = b[lo + SUB:lo + SUB + 1] if rev else b[lo - 1:lo]
                valid = (rows_c >= lo + SUB) if rev else (rows_c < lo)
                qt = qi * jnp.exp(bi - m)
                kt = k * jnp.exp(jnp.where(valid, m - b, NEG_BIG))
                qs = jnp.concatenate([qt * hm for hm in hmasks], axis=0)
                a = _dot_nt(qs.astype(BF16), kt.astype(BF16))
                oo = _dot(a.astype(BF16), vb)
                for hh in range(GROUP_HEADS):
                    acc = acc + oo[hh * SUB:(hh + 1) * SUB] * hmasks[hh]
            outs.append(acc)
        o_ref[0, pl.ds(row0, CHUNK), :] += jnp.concatenate(outs, axis=0)

    kh = k * jnp.exp(b_end - b)
    st_ref[...] = st * jnp.exp(b_end) + _dot_tn(vb, kh.astype(BF16)) * bmask


def _hgrn_kernel(za_ref, lg_ref, gain_ref, o_ref, stf_ref, stb_ref, *, layer, need_ctx):
    t = za_ref.shape[1]
    n_tiles = t // ROW_TILE
    per_tile = ROW_TILE // CHUNK
    lg = lg_ref[...]
    depth = lg.shape[0]
    mx = lg[0]
    for i in range(1, depth):
        mx = jnp.maximum(mx, lg[i])
    es = [jnp.exp(lg[i] - mx) for i in range(depth)]
    tot = es[0]
    for i in range(1, depth):
        tot = tot + es[i]
    sm = [e / tot for e in es]
    cs = sm[0]
    for i in range(1, layer + 1):
        cs = cs + sm[i]
    lbs = cs - sm[0]
    lb_f, lb_b = lbs[0:1], lbs[1:2]

    ones_bd = _head_block_ones(GROUP_WIDTH, GROUP_WIDTH, BF16)
    bmask = _head_block_ones(GROUP_WIDTH, GROUP_WIDTH, F32)
    lane_head = _iota((1, GROUP_WIDTH), 1) >> 6
    hmasks = [jnp.where(lane_head == hh, 1.0, 0.0).astype(F32) for hh in range(GROUP_HEADS)]
    consts = (ones_bd, bmask, hmasks)
    r4, c4 = _iota((ROW_TILE, ROW_TILE), 0), _iota((ROW_TILE, ROW_TILE), 1)
    same_chunk = (r4 >> 6) == (c4 >> 6)
    cum_f = jnp.where(same_chunk & (c4 <= r4), 1.0, 0.0).astype(BF16)
    cum_b = jnp.where(same_chunk & (c4 >= r4), 1.0, 0.0).astype(BF16)

    stf_ref[...] = jnp.zeros_like(stf_ref)
    stb_ref[...] = jnp.zeros_like(stb_ref)
    o_ref[...] = jnp.zeros_like(o_ref)

    def tile_step(row_f, row_b, want):
        k_f, b_f = _hgrn_tile_gates(za_ref, row_f, lb_f, False, cum_f)
        k_b, b_b = _hgrn_tile_gates(za_ref, row_b, lb_b, True, cum_b)
        for c in range(per_tile):
            lo_f = c * CHUNK
            lo_b = (per_tile - 1 - c) * CHUNK
            _hgrn_chunk(za_ref, o_ref, stf_ref, row_f + lo_f, k_f[lo_f:lo_f + CHUNK], b_f[lo_f:lo_f + CHUNK],
                        False, want, consts)
            _hgrn_chunk(za_ref, o_ref, stb_ref, row_b + lo_b, k_b[lo_b:lo_b + CHUNK], b_b[lo_b:lo_b + CHUNK],
                        True, want, consts)

    n_ctx_tiles = CTX_LEN // ROW_TILE
    for i in range(n_ctx_tiles):
        tile_step(i * ROW_TILE, (n_ctx_tiles - 1 - i) * ROW_TILE, need_ctx)

    def lat_body(i, carry):
        tile_step(pl.multiple_of((n_ctx_tiles + i) * ROW_TILE, ROW_TILE),
                  pl.multiple_of((n_tiles - 1 - i) * ROW_TILE, ROW_TILE), True)
        return carry

    lax.fori_loop(0, n_tiles - n_ctx_tiles, lat_body, 0)

    gain = gain_ref[...]

    def fin_body(i, carry):
        r0 = pl.multiple_of(i * ROW_TILE, ROW_TILE)
        o = o_ref[0, pl.ds(r0, ROW_TILE), :]
        zo = za_ref[0, pl.ds(r0, ROW_TILE), 2 * GROUP_WIDTH:3 * GROUP_WIDTH]
        o_ref[0, pl.ds(r0, ROW_TILE), :] = _head_rms(o, gain, ones_bd) * (zo * _sigmoid(zo))
        return carry

    lax.fori_loop(0, t // ROW_TILE, fin_body, 0)


def _hgrn(za, lb_logits, gain4, layer, need_ctx):
    b, t, _ = za.shape
    depth = lb_logits.shape[0]
    return pl.pallas_call(
        functools.partial(_hgrn_kernel, layer=layer, need_ctx=need_ctx),
        grid=(b,),
        in_specs=[
            pl.BlockSpec((1, t, A_W), lambda bb: (bb, 0, 0)),
            pl.BlockSpec((depth, 2, GROUP_WIDTH), lambda bb: (0, 0, 0)),
            pl.BlockSpec((1, GROUP_WIDTH), lambda bb: (0, 0)),
        ],
        out_specs=pl.BlockSpec((1, t, GROUP_WIDTH), lambda bb: (bb, 0, 0)),
        out_shape=jax.ShapeDtypeStruct((b, t, GROUP_WIDTH), F32),
        scratch_shapes=[pltpu.VMEM((GROUP_WIDTH, GROUP_WIDTH), F32),
                        pltpu.VMEM((GROUP_WIDTH, GROUP_WIDTH), F32)],
        compiler_params=pltpu.CompilerParams(
            dimension_semantics=("parallel",), vmem_limit_bytes=VMEM_LIMIT_BYTES),
        name="hgrn2_scan",
    )(za, lb_logits, gain4)


def _mlstm_tile_gates(zg_ref, row0, gate_b, consts):
    cmat4, e_i, e_f = consts[0:3]
    hi, mid, lo = _split3(zg_ref[0, pl.ds(row0, ROW_TILE), :] + gate_b)
    ig = _dot(hi, e_i) + _dot(mid, e_i) + _dot(lo, e_i)
    lf = _log_sigmoid(_dot(hi, e_f) + _dot(mid, e_f) + _dot(lo, e_f))
    return ig, _route_l(cmat4, lf)


def _mlstm_chunk(zb_ref, o_ref, ct_ref, n_ref, m_ref, row0, ig, bc, rev, want_out, consts):
    ones_bd, bmask, diag, causal = consts[3:7]
    q = zb_ref[0, pl.ds(row0, CHUNK), 0:GROUP_WIDTH]
    k = zb_ref[0, pl.ds(row0, CHUNK), GROUP_WIDTH:2 * GROUP_WIDTH] * (HEAD_DIM ** -0.5)
    v = zb_ref[0, pl.ds(row0, CHUNK), 2 * GROUP_WIDTH:3 * GROUP_WIDTH]
    b_end = bc[0:1] if rev else bc[CHUNK - 1:CHUNK]
    m_prev = m_ref[0:1]
    n_prev = n_ref[0:1]
    ct = ct_ref[...]
    vb = v.astype(BF16)

    if want_out:
        qb = q.astype(BF16)
        rowv = jnp.sum((ig - bc) * diag, axis=0, keepdims=True)
        dlog = jnp.where(causal, bc + rowv, NEG_BIG)
        segs = []
        for hh in range(GROUP_HEADS):
            mh = jnp.max(dlog[:, hh * HEAD_DIM:(hh + 1) * HEAD_DIM], axis=-1, keepdims=True)
            segs.append(jnp.broadcast_to(mh, (CHUNK, HEAD_DIM)))
        m_loc = jnp.concatenate(segs, axis=-1)
        kk = jnp.concatenate([k.astype(BF16)] * GROUP_HEADS, axis=0) * ones_bd
        vv = jnp.concatenate([vb] * GROUP_HEADS, axis=0) * ones_bd
        pb = (jnp.exp(dlog - m_loc) * _dot_nt(qb, kk)).astype(BF16)
        num_loc = _dot(pb, vv)
        den_loc = _dot(pb, ones_bd)
        qc = _dot_nt(qb, ct.astype(BF16))
        qn = _dot((q * n_prev).astype(BF16), ones_bd)
        a = bc + m_prev
        m_t = jnp.maximum(a, m_loc)
        w_in = jnp.exp(a - m_t)
        w_loc = jnp.exp(m_loc - m_t)
        num = w_in * qc + w_loc * num_loc
        den = w_in * qn + w_loc * den_loc
        o_ref[0, pl.ds(row0, CHUNK), :] += num / jnp.maximum(jnp.abs(den), jnp.exp(-m_t))

    gs = b_end - bc + ig
    g_max = jnp.max(gs, axis=0, keepdims=True)
    kw = k * jnp.exp(gs - g_max)
    m_new = jnp.maximum(b_end + m_prev, g_max)
    w_old = jnp.exp(b_end + m_prev - m_new)
    w_upd = jnp.exp(g_max - m_new)
    ct_ref[...] = ct * w_old + _dot_tn(vb, kw.astype(BF16)) * (bmask * w_upd)
    n_ref[...] = jnp.broadcast_to(n_prev * w_old + jnp.sum(kw, axis=0, keepdims=True) * w_upd, n_ref.shape)
    m_ref[...] = jnp.broadcast_to(m_new, m_ref.shape)


def _mlstm_kernel(zb_ref, zg_ref, gb_ref, gain_ref, o_ref, ctf_ref, ctb_ref, nf_ref, nb_ref,
                  mf_ref, mb_ref, *, need_ctx):
    t = zb_ref.shape[1]
    n_tiles = t // ROW_TILE
    per_tile = ROW_TILE // CHUNK
    ones_bd = _head_block_ones(GROUP_WIDTH, GROUP_WIDTH, BF16)
    bmask = _head_block_ones(GROUP_WIDTH, GROUP_WIDTH, F32)
    gcol = _iota((128, GROUP_WIDTH), 0)
    ghead = _iota((128, GROUP_WIDTH), 1) >> 6
    route = lambda base: jnp.where(gcol == base + ghead, 1.0, 0.0).astype(BF16)
    tt = _iota((CHUNK, GROUP_WIDTH), 0)
    ss = _iota((CHUNK, GROUP_WIDTH), 1) & (HEAD_DIM - 1)
    diag = jnp.where(tt == ss, 1.0, 0.0).astype(F32)
    r4, c4 = _iota((ROW_TILE, ROW_TILE), 0), _iota((ROW_TILE, ROW_TILE), 1)
    same_chunk = (r4 >> 6) == (c4 >> 6)
    cum4 = lambda rev: jnp.where(same_chunk & ((c4 >= r4) if rev else (c4 <= r4)), 1.0, 0.0).astype(BF16)
    cf = (cum4(False), route(0), route(2 * GROUP_HEADS), ones_bd, bmask, diag, ss <= tt)
    cb = (cum4(True), route(GROUP_HEADS), route(3 * GROUP_HEADS), ones_bd, bmask, diag, ss >= tt)
    gate_b = gb_ref[...]

    for r in (ctf_ref, ctb_ref, nf_ref, nb_ref, mf_ref, mb_ref, o_ref):
        r[...] = jnp.zeros_like(r)

    def tile_step(row_f, row_b, want):
        ig_f, bc_f = _mlstm_tile_gates(zg_ref, row_f, gate_b, cf)
        ig_b, bc_b = _mlstm_tile_gates(zg_ref, row_b, gate_b, cb)
        for c in range(per_tile):
            lo_f = c * CHUNK
            lo_b = (per_tile - 1 - c) * CHUNK
            _mlstm_chunk(zb_ref, o_ref, ctf_ref, nf_ref, mf_ref, row_f + lo_f,
                         ig_f[lo_f:lo_f + CHUNK], bc_f[lo_f:lo_f + CHUNK], False, want, cf)
            _mlstm_chunk(zb_ref, o_ref, ctb_ref, nb_ref, mb_ref, row_b + lo_b,
                         ig_b[lo_b:lo_b + CHUNK], bc_b[lo_b:lo_b + CHUNK], True, want, cb)

    n_ctx_tiles = CTX_LEN // ROW_TILE
    for i in range(n_ctx_tiles):
        tile_step(i * ROW_TILE, (n_ctx_tiles - 1 - i) * ROW_TILE, need_ctx)

    def lat_body(i, carry):
        tile_step(pl.multiple_of((n_ctx_tiles + i) * ROW_TILE, ROW_TILE),
                  pl.multiple_of((n_tiles - 1 - i) * ROW_TILE, ROW_TILE), True)
        return carry

    lax.fori_loop(0, n_tiles - n_ctx_tiles, lat_body, 0)

    gain = gain_ref[...]

    def fin_body(i, carry):
        r0 = pl.multiple_of(i * ROW_TILE, ROW_TILE)
        o = o_ref[0, pl.ds(r0, ROW_TILE), :]
        zo = zb_ref[0, pl.ds(r0, ROW_TILE), 3 * GROUP_WIDTH:4 * GROUP_WIDTH]
        o_ref[0, pl.ds(r0, ROW_TILE), :] = _head_rms(o, gain, ones_bd) * _sigmoid(zo)
        return carry

    lax.fori_loop(0, t // ROW_TILE, fin_body, 0)


def _mlstm(zb, zg, gate_b128, gain4, need_ctx):
    b, t, _ = zb.shape
    return pl.pallas_call(
        functools.partial(_mlstm_kernel, need_ctx=need_ctx),
        grid=(b,),
        in_specs=[
            pl.BlockSpec((1, t, B_W), lambda bb: (bb, 0, 0)),
            pl.BlockSpec((1, t, G_W), lambda bb: (bb, 0, 0)),
            pl.BlockSpec((1, G_W), lambda bb: (0, 0)),
            pl.BlockSpec((1, GROUP_WIDTH), lambda bb: (0, 0)),
        ],
        out_specs=pl.BlockSpec((1, t, GROUP_WIDTH), lambda bb: (bb, 0, 0)),
        out_shape=jax.ShapeDtypeStruct((b, t, GROUP_WIDTH), F32),
        scratch_shapes=[pltpu.VMEM((GROUP_WIDTH, GROUP_WIDTH), F32),
                        pltpu.VMEM((GROUP_WIDTH, GROUP_WIDTH), F32),
                        pltpu.VMEM((SUB, GROUP_WIDTH), F32), pltpu.VMEM((SUB, GROUP_WIDTH), F32),
                        pltpu.VMEM((SUB, GROUP_WIDTH), F32), pltpu.VMEM((SUB, GROUP_WIDTH), F32)],
        compiler_params=pltpu.CompilerParams(
            dimension_semantics=("parallel",), vmem_limit_bytes=VMEM_LIMIT_BYTES),
        name="mlstm_scan",
    )(zb, zg, gate_b128, gain4)


def _attend_t(qt, pieces):
    s_list = []
    for k, _, bias in pieces:
        s = _dot(k, qt)
        s_list.append(s if bias is None else s + bias)
    m = jnp.max(s_list[0], axis=0, keepdims=True)
    for s in s_list[1:]:
        m = jnp.maximum(m, jnp.max(s, axis=0, keepdims=True))
    l = acc = None
    for s, (_, vt, _) in zip(s_list, pieces):
        p = jnp.exp2(s - m)
        ls = jnp.sum(p, axis=0, keepdims=True)
        pv = _dot(vt, p.astype(BF16))
        l = ls if l is None else l + ls
        acc = pv if acc is None else acc + pv
    return acc / l


def _gqa_kernel(q_ref, k_ref, vt_ref, o_ref, *, need_ctx):
    j = pl.program_id(2)
    t = k_ref.shape[2]
    g = q_ref.shape[1]

    def attend(nk):
        qt = jnp.concatenate([q_ref[0, i] for i in range(g)], axis=1)
        ot = _attend_t(qt, [(k_ref[0, 0, 0:nk, :], vt_ref[0, 0, :, 0:nk], None)])
        o_ref[0] = jnp.concatenate([ot[:, i * ROW_TILE:(i + 1) * ROW_TILE] for i in range(g)], axis=0).T

    @pl.when(j == 0)
    def _():
        if need_ctx:
            attend(CTX_LEN)
        else:
            o_ref[...] = jnp.zeros_like(o_ref)

    @pl.when(j > 0)
    def _():
        attend(t)


def _gqa(qct, kc, vct, need_ctx):
    b, _, t, _ = kc.shape
    nq = t // ROW_TILE
    g = GROUP_HEADS // KV_HEADS
    return pl.pallas_call(
        functools.partial(_gqa_kernel, need_ctx=need_ctx),
        grid=(b, KV_HEADS, nq),
        in_specs=[
            pl.BlockSpec((1, g, HEAD_DIM, ROW_TILE), lambda bb, h, j: (bb, h, 0, j)),
            pl.BlockSpec((1, 1, t, HEAD_DIM), lambda bb, h, j: (bb, h, 0, 0)),
            pl.BlockSpec((1, 1, HEAD_DIM, t), lambda bb, h, j: (bb, h, 0, 0)),
        ],
        out_specs=pl.BlockSpec((1, ROW_TILE, g * HEAD_DIM), lambda bb, h, j: (bb, j, h)),
        out_shape=jax.ShapeDtypeStruct((b, t, GROUP_WIDTH), F32),
        compiler_params=pltpu.CompilerParams(
            dimension_semantics=("parallel", "parallel", "arbitrary"), vmem_limit_bytes=VMEM_LIMIT_BYTES),
        name="gqa_attention",
    )(qct, kc, vct)


def _na_plan(rows):
    kh = min(NA_KH, rows)
    win = min(kh + NA_QROWS, rows)
    assert rows % NA_QROWS == 0 and (win * GRID_W) % 128 == 0
    blocks, variants = [], []
    for i in range(rows // NA_QROWS):
        qrows = np.arange(NA_QROWS * i, NA_QROWS * (i + 1))
        r0 = np.clip(qrows - kh // 2, 0, rows - kh)
        start = int(np.clip(NA_QROWS * i - kh // 2, 0, rows - win))
        assert start % 2 == 0 and start <= r0.min() and r0.max() + kh <= start + win
        key = (start - NA_QROWS * i, tuple(int(v) for v in r0 - qrows))
        if key not in variants:
            variants.append(key)
        blocks.append((start, variants.index(key)))
    return kh, win, tuple(blocks), variants


def _na_kernel(q_ref, k_ref, vt_ref, bias_ref, o_ref, *, need_ctx, blocks):
    nq = NA_QROWS * GRID_W
    nk = bias_ref.shape[2]
    kctx = k_ref[0, 0, 0:CTX_LEN, :]
    vtctx = vt_ref[0, 0, :, 0:CTX_LEN]

    def paired_queries(lo, n):
        q = q_ref[0, 0, :, lo:lo + n]
        z = jnp.zeros((HEAD_DIM, n), q.dtype)
        return jnp.concatenate([jnp.concatenate([q[0:HEAD_DIM], z], axis=1),
                                jnp.concatenate([z, q[HEAD_DIM:2 * HEAD_DIM]], axis=1)], axis=0)

    def unpair(ot, n):
        return jnp.concatenate([ot[0:HEAD_DIM, 0:n], ot[HEAD_DIM:2 * HEAD_DIM, n:2 * n]], axis=0).T

    if need_ctx:
        ot = _attend_t(paired_queries(0, CTX_LEN), [(kctx, vtctx, None)])
        o_ref[0, 0:CTX_LEN, :] = unpair(ot, CTX_LEN)
    else:
        o_ref[0, 0:CTX_LEN, :] = jnp.zeros((CTX_LEN, 2 * HEAD_DIM), F32)

    for i, (start, var) in enumerate(blocks):
        q0 = CTX_LEN + i * nq
        k0 = CTX_LEN + start * GRID_W
        pieces = [(kctx, vtctx, None),
                  (k_ref[0, 0, k0:k0 + nk, :], vt_ref[0, 0, :, k0:k0 + nk], bias_ref[0, var])]
        o_ref[0, q0:q0 + nq, :] = unpair(_attend_t(paired_queries(q0, nq), pieces), nq)


def _na(qdt, kd, vdt, bias_t, blocks, need_ctx):
    b, npair, t, pw = kd.shape
    nvar, nk, nq2 = bias_t.shape[1:]
    blk = pl.BlockSpec((1, 1, t, pw), lambda h, bb: (bb, h, 0, 0))
    blk_t = pl.BlockSpec((1, 1, pw, t), lambda h, bb: (bb, h, 0, 0))
    return pl.pallas_call(
        functools.partial(_na_kernel, need_ctx=need_ctx, blocks=blocks),
        grid=(npair, b),
        in_specs=[blk_t, blk, blk_t,
                  pl.BlockSpec((1, nvar, nk, nq2), lambda h, bb: (h, 0, 0, 0))],
        out_specs=pl.BlockSpec((1, t, pw), lambda h, bb: (bb, 0, h)),
        out_shape=jax.ShapeDtypeStruct((b, t, GROUP_WIDTH), F32),
        compiler_params=pltpu.CompilerParams(
            dimension_semantics=("parallel", "arbitrary"), vmem_limit_bytes=VMEM_LIMIT_BYTES),
        name="neighbourhood_attention",
    )(qdt, kd, vdt, bias_t)


def _na_bias_table(rpb, kh, win, variants):
    rows_q = NA_QROWS
    col = np.arange(GRID_W)
    cstart = np.clip(col - NA_KW // 2, 0, GRID_W - NA_KW)
    col_in = (col[None, :] >= cstart[:, None]) & (col[None, :] < cstart[:, None] + NA_KW)
    col_idx = np.clip(col[None, :] - col[:, None], 1 - NA_KW, NA_KW - 1) + NA_KW - 1
    tables = []
    for ds, offs in variants:
        rel = ds + np.arange(win)[:, None] - np.arange(rows_q)[None, :]
        offs = np.asarray(offs)[None, :]
        valid = (rel >= offs) & (rel < offs + kh)
        ridx = np.clip(rel + NA_KH - 1, 0, 2 * NA_KH - 2)
        tbl = rpb[:, ridx][:, :, :, col_idx].astype(F32) * LOG2E
        keep = valid[None, :, :, None, None] & col_in[None, None, None]
        tbl = jnp.where(keep, tbl, NEG_BIG).transpose(0, 1, 4, 2, 3)
        tbl = tbl.reshape(rpb.shape[0] // 2, 2, win * GRID_W, rows_q * GRID_W)
        tables.append(tbl.transpose(0, 2, 1, 3).reshape(rpb.shape[0] // 2, win * GRID_W, 2 * rows_q * GRID_W))
    return jnp.stack(tables, axis=1)


def _outmlp_kernel(x_ref, a_ref, b_ref, c_ref, d_ref, mlat_ref, mctx_ref, g2_ref, wo_ref, w1_ref,
                   w2_ref, gf_ref, o_ref, *, tile_off, final):
    j = pl.program_id(1) + tile_off
    mod = jnp.where(j == 0, mctx_ref[0], mlat_ref[0])
    cat = jnp.concatenate([a_ref[0], b_ref[0], c_ref[0], d_ref[0]], axis=-1).astype(BF16)
    x1 = x_ref[0] + mod[2:3] * _dot(cat, wo_ref[...])
    ms = jnp.mean(x1 * x1, axis=-1, keepdims=True)
    h2 = ((x1 * lax.rsqrt(ms + EPS) * g2_ref[...]) * (1.0 + mod[4:5]) + mod[3:4]).astype(BF16)
    hid = w1_ref.shape[1]
    step = 1024
    acc = jnp.zeros(x1.shape, F32)
    for c in range(hid // step):
        u = jnp.maximum(_dot(h2, w1_ref[:, c * step:(c + 1) * step]), 0.0)
        acc = acc + _dot((u * u).astype(BF16), w2_ref[c * step:(c + 1) * step, :])
    x2 = x1 + mod[5:6] * acc
    if final:
        ms2 = jnp.mean(x2 * x2, axis=-1, keepdims=True)
        x2 = x2 * lax.rsqrt(ms2 + EPS) * gf_ref[...]
    o_ref[0] = x2


def _outmlp(xcat, a, bm, c, dd, mods_l, g2, wo, w1, w2, gf, final):
    b, t, d = xcat.shape
    tile_off = CTX_LEN // ROW_TILE if final else 0
    nt = t // ROW_TILE - tile_off
    ctx_row = mods_l.shape[0] - 1
    full2 = lambda bb, j: (0, 0)
    grp = pl.BlockSpec((1, ROW_TILE, GROUP_WIDTH), lambda bb, j: (bb, j + tile_off, 0))
    return pl.pallas_call(
        functools.partial(_outmlp_kernel, tile_off=tile_off, final=final),
        grid=(b, nt),
        in_specs=[
            pl.BlockSpec((1, ROW_TILE, d), lambda bb, j: (bb, j + tile_off, 0)),
            grp, grp, grp, grp,
            pl.BlockSpec((1, 6, d), lambda bb, j: (bb, 0, 0)),
            pl.BlockSpec((1, 6, d), lambda bb, j: (ctx_row, 0, 0)),
            pl.BlockSpec((1, d), full2),
            pl.BlockSpec(wo.shape, full2),
            pl.BlockSpec(w1.shape, full2),
            pl.BlockSpec(w2.shape, full2),
            pl.BlockSpec((1, d), full2),
        ],
        out_specs=pl.BlockSpec((1, ROW_TILE, d), lambda bb, j: (bb, j, 0)),
        out_shape=jax.ShapeDtypeStruct((b, nt * ROW_TILE, d), F32),
        compiler_params=pltpu.CompilerParams(
            dimension_semantics=("parallel", "arbitrary"), vmem_limit_bytes=VMEM_LIMIT_BYTES),
        name="out_projection_mlp",
    )(xcat, a, bm, c, dd, mods_l, mods_l, g2, wo, w1, w2, gf)


def _pack_in_projection(w):
    gw, kvw = GROUP_WIDTH, KV_HEADS * HEAD_DIM
    g0 = 9 * gw
    c0 = g0 + 4 * GROUP_HEADS
    d0 = c0 + gw + 2 * kvw

    def pair_swapped(cols):
        n = cols.shape[1]
        return cols.reshape(cols.shape[0], n // 2, 2)[:, :, ::-1].reshape(cols.shape[0], n)

    q, k = w[:, c0:c0 + gw], w[:, c0 + gw:c0 + gw + kvw]
    parts = [w[:, 0:g0], w[:, g0:c0], jnp.zeros((w.shape[0], G_W - 4 * GROUP_HEADS), w.dtype),
             q, pair_swapped(q), k, pair_swapped(k), w[:, c0 + gw + kvw:d0], w[:, d0:d0 + 3 * gw]]
    packed = jnp.concatenate(parts, axis=1).astype(BF16)
    assert packed.shape[1] == IN_PACKED
    return packed


def _rope_tables(n_tokens):
    t = np.arange(n_tokens)
    row = (t // GRID_W).astype(np.float32)
    col = (t % GRID_W).astype(np.float32)
    axis_dims = HEAD_DIM // 2
    inv = np.power(np.float32(ROPE_THETA),
                   (-2.0 * np.arange(axis_dims // 2, dtype=np.float32) / axis_dims).astype(np.float32))
    ang = np.concatenate([row[:, None] * inv, col[:, None] * inv], axis=-1).astype(np.float32)
    cos = np.repeat(np.cos(ang).astype(np.float32), 2, axis=-1)
    sign = np.where(np.arange(HEAD_DIM) % 2 == 0, -1.0, 1.0).astype(np.float32)
    sin = np.repeat(np.sin(ang).astype(np.float32), 2, axis=-1) * sign
    cos = np.concatenate([np.ones((CTX_LEN, HEAD_DIM), np.float32), cos], axis=0)
    sin = np.concatenate([np.zeros((CTX_LEN, HEAD_DIM), np.float32), sin], axis=0)
    return jnp.asarray(np.tile(cos, (1, GROUP_HEADS))), jnp.asarray(np.tile(sin, (1, GROUP_HEADS)))


def kernel(x, c, ctx, c_ctx, w_mod, b_mod, norm1_g, norm2_g, w_in, hgrn_lb_logits, hgrn_norm_g,
           mlstm_gate_b, mlstm_norm_g, gqa_qnorm_g, gqa_knorm_g, na_rpb, w_out, w_mlp1, w_mlp2,
           final_norm_g):
    bsz, seq, d = x.shape
    depth = w_mod.shape[0]
    rows = seq // GRID_W

    n_c = bsz + 1
    pad = (-n_c) % 8
    cc = jnp.concatenate([c, c_ctx[None, :], jnp.zeros((pad, d), F32)], axis=0)
    mods = _modulation(cc, w_mod, b_mod)[:, :n_c].reshape(depth, n_c, 6, d)

    cos4, sin4 = _rope_tables(seq)
    swap = np.arange(HEAD_DIM) ^ 1

    xcat = jnp.concatenate([ctx, x], axis=1)
    out = None
    for l in range(depth):
        need_ctx = l < depth - 1
        w_packed = _pack_in_projection(w_in[l])
        gq = jnp.tile(gqa_qnorm_g[l], GROUP_HEADS)[None, :]
        gqs = jnp.tile(gqa_qnorm_g[l][swap], GROUP_HEADS)[None, :]
        gk = jnp.tile(gqa_knorm_g[l], KV_HEADS)[None, :]
        gks = jnp.tile(gqa_knorm_g[l][swap], KV_HEADS)[None, :]
        za, zb, zg, qc, kc, vc, qd, kd, vd = _inproj(
            xcat, mods[l], norm1_g[l][None, :], w_packed, cos4, sin4, gq, gqs, gk, gks)

        a = _hgrn(za, hgrn_lb_logits, jnp.tile(hgrn_norm_g[l], GROUP_HEADS)[None, :], l, need_ctx)
        gate_b = jnp.concatenate([mlstm_gate_b[l], jnp.zeros((G_W - 4 * GROUP_HEADS,), F32)])[None, :]
        bm = _mlstm(zb, zg, gate_b, jnp.tile(mlstm_norm_g[l], GROUP_HEADS)[None, :], need_ctx)
        cg = _gqa(qc, kc, vc, need_ctx)
        kh, win, blocks, variants = _na_plan(rows)
        dg = _na(qd, kd, vd, _na_bias_table(na_rpb[l], kh, win, variants), blocks, need_ctx)

        res = _outmlp(xcat, a, bm, cg, dg, mods[l], norm2_g[l][None, :], w_out[l].astype(BF16),
                      w_mlp1[l].astype(BF16), w_mlp2[l].astype(BF16), final_norm_g[None, :],
                      final=not need_ctx)
        if need_ctx:
            xcat = res
        else:
            out = res
    return out
```

```python
import functools

import numpy as np
import jax
import jax.numpy as jnp
from jax import lax
from jax.experimental import pallas as pl
from jax.experimental.pallas import tpu as pltpu

F32 = jnp.float32
BF16 = jnp.bfloat16

D_MODEL = 1024
GRID_W = 64
CTX_LEN = 256
HEAD_DIM = 64
GROUP_HEADS = 4
GROUP_WIDTH = GROUP_HEADS * HEAD_DIM
KV_HEADS = 2
MLP_HIDDEN = 4 * D_MODEL
CHUNK = 64
SUB = 8
NA_KH = 8
NA_KW = 16
ROPE_THETA = 10000.0
EPS = 1e-6
NEG_BIG = -1e30
FORGET_FLOOR = 1e-20
ROW_TILE = 256
VMEM_LIMIT_BYTES = 56 * 1024 * 1024
LOG2E = 1.4426950408889634
Q_SCALE = HEAD_DIM ** -0.5 * LOG2E
NA_QROWS = 4

A_OFF, A_W = 0, 5 * GROUP_WIDTH
B_OFF, B_W = A_OFF + A_W, 4 * GROUP_WIDTH
G_OFF, G_W = B_OFF + B_W, 128
C_OFF, C_W = G_OFF + G_W, 896
D_OFF, D_W = C_OFF + C_W, 3 * GROUP_WIDTH
IN_PACKED = D_OFF + D_W


def _dot(a, b):
    return jnp.dot(a, b, preferred_element_type=F32)


def _dot_nt(a, b):
    return lax.dot_general(a, b, (((1,), (1,)), ((), ())), preferred_element_type=F32)


def _dot_tn(a, b):
    return lax.dot_general(a, b, (((0,), (0,)), ((), ())), preferred_element_type=F32)


def _split3(x):
    hi = x.astype(BF16)
    r = x - hi.astype(F32)
    mid = r.astype(BF16)
    lo = (r - mid.astype(F32)).astype(BF16)
    return hi, mid, lo


def _route_l(m01, x):
    hi, mid, lo = _split3(x)
    return _dot(m01, hi) + _dot(m01, mid) + _dot(m01, lo)


def _route_r(x, m01):
    hi, mid, lo = _split3(x)
    return _dot(hi, m01) + _dot(mid, m01) + _dot(lo, m01)


def _sigmoid(x):
    return 1.0 / (1.0 + jnp.exp(-x))


def _log_sigmoid(x):
    return -(jnp.maximum(-x, 0.0) + jnp.log1p(jnp.exp(-jnp.abs(x))))


def _iota(shape, dim):
    return lax.broadcasted_iota(jnp.int32, shape, dim)


def _head_block_ones(n, m, dtype):
    return jnp.where((_iota((n, m), 0) >> 6) == (_iota((n, m), 1) >> 6), 1.0, 0.0).astype(dtype)


def _cumsum_matrix(rev):
    r, c = _iota((CHUNK, CHUNK), 0), _iota((CHUNK, CHUNK), 1)
    keep = (c >= r) if rev else (c <= r)
    return jnp.where(keep, 1.0, 0.0).astype(BF16)


def _head_rms(o, gain, ones_bd):
    ssq = _route_r(o * o, ones_bd)
    return o * lax.rsqrt(ssq * (1.0 / HEAD_DIM) + EPS) * gain


def _mod_kernel(c_ref, w_ref, b_ref, o_ref):
    c = c_ref[...]
    s = c * _sigmoid(c)
    o_ref[0] = _dot(s.astype(BF16), w_ref[0].astype(BF16)) + b_ref[0]


def _modulation(cc, w_mod, b_mod):
    depth, d, n = w_mod.shape
    rows = cc.shape[0]
    tn = 1024
    return pl.pallas_call(
        _mod_kernel,
        grid=(depth, n // tn),
        in_specs=[
            pl.BlockSpec((rows, d), lambda l, j: (0, 0)),
            pl.BlockSpec((1, d, tn), lambda l, j: (l, 0, j)),
            pl.BlockSpec((1, 1, tn), lambda l, j: (l, 0, j)),
        ],
        out_specs=pl.BlockSpec((1, rows, tn), lambda l, j: (l, 0, j)),
        out_shape=jax.ShapeDtypeStruct((depth, rows, n), F32),
        compiler_params=pltpu.CompilerParams(
            dimension_semantics=("arbitrary", "arbitrary"), vmem_limit_bytes=VMEM_LIMIT_BYTES),
        name="adaln_modulation",
    )(cc, w_mod, b_mod.reshape(depth, 1, n))


def _inproj_kernel(x_ref, mlat_ref, mctx_ref, g1_ref, w_ref, cos_ref, sin_ref, gq_ref, gqs_ref,
                   gk_ref, gks_ref, za_ref, zb_ref, zg_ref, qc_ref, kc_ref, vc_ref,
                   qd_ref, kd_ref, vd_ref):
    j = pl.program_id(1)
    mod = jnp.where(j == 0, mctx_ref[0], mlat_ref[0])
    x = x_ref[0]
    ms = jnp.mean(x * x, axis=-1, keepdims=True)
    y = x * lax.rsqrt(ms + EPS) * g1_ref[...]
    h = (y * (1.0 + mod[1:2]) + mod[0:1]).astype(BF16)

    za_ref[0] = _dot(h, w_ref[:, A_OFF:A_OFF + A_W])
    zb_ref[0] = _dot(h, w_ref[:, B_OFF:B_OFF + B_W])
    zg_ref[0] = _dot(h, w_ref[:, G_OFF:G_OFF + G_W])

    zc = _dot(h, w_ref[:, C_OFF:C_OFF + C_W])
    ones4 = _head_block_ones(GROUP_WIDTH, GROUP_WIDTH, BF16)
    cos, sin = cos_ref[...], sin_ref[...]
    q, qs = zc[:, 0:256], zc[:, 256:512]
    rq = lax.rsqrt(_route_r(q * q, ones4) * (1.0 / HEAD_DIM) + EPS)
    qrt = ((rq * (q * gq_ref[...] * cos + qs * gqs_ref[...] * sin)) * Q_SCALE).T
    k, ks = zc[:, 512:640], zc[:, 640:768]
    rk = lax.rsqrt(_route_r(k * k, ones4[0:128, 0:128]) * (1.0 / HEAD_DIM) + EPS)
    kr = (rk * (k * gk_ref[...] * cos[:, 0:128] + ks * gks_ref[...] * sin[:, 0:128])).astype(BF16)
    vvt = zc[:, 768:896].T
    for hh in range(GROUP_HEADS):
        qc_ref[0, hh] = qrt[hh * HEAD_DIM:(hh + 1) * HEAD_DIM].astype(BF16)
    for hh in range(KV_HEADS):
        kc_ref[0, hh] = kr[:, hh * HEAD_DIM:(hh + 1) * HEAD_DIM]
        vc_ref[0, hh] = vvt[hh * HEAD_DIM:(hh + 1) * HEAD_DIM].astype(BF16)

    zd = _dot(h, w_ref[:, D_OFF:D_OFF + D_W])
    qdt = (zd[:, 0:256] * Q_SCALE).T
    kdb = zd[:, 256:512].astype(BF16)
    vdt = zd[:, 512:768].T
    for pp in range(GROUP_HEADS // 2):
        lo, hi = pp * 2 * HEAD_DIM, (pp + 1) * 2 * HEAD_DIM
        qd_ref[0, pp] = qdt[lo:hi].astype(BF16)
        kd_ref[0, pp] = kdb[:, lo:hi]
        vd_ref[0, pp] = vdt[lo:hi].astype(BF16)


def _inproj(xcat, mods_l, g1, w_packed, cos4, sin4, gq, gqs, gk, gks):
    b, t, d = xcat.shape
    nt = t // ROW_TILE
    ctx_row = mods_l.shape[0] - 1
    full2 = lambda bb, j: (0, 0)
    headed = lambda nh, w=HEAD_DIM: pl.BlockSpec((1, nh, ROW_TILE, w), lambda bb, j: (bb, 0, j, 0))
    headed_t = lambda nh, w=HEAD_DIM: pl.BlockSpec((1, nh, w, ROW_TILE), lambda bb, j: (bb, 0, 0, j))
    flat = lambda w: pl.BlockSpec((1, ROW_TILE, w), lambda bb, j: (bb, j, 0))
    hshape = lambda nh, w=HEAD_DIM: jax.ShapeDtypeStruct((b, nh, t, w), BF16)
    hshape_t = lambda nh, w=HEAD_DIM: jax.ShapeDtypeStruct((b, nh, w, t), BF16)
    fshape = lambda w: jax.ShapeDtypeStruct((b, t, w), F32)
    pw = 2 * HEAD_DIM
    return pl.pallas_call(
        _inproj_kernel,
        grid=(b, nt),
        in_specs=[
            pl.BlockSpec((1, ROW_TILE, d), lambda bb, j: (bb, j, 0)),
            pl.BlockSpec((1, 6, d), lambda bb, j: (bb, 0, 0)),
            pl.BlockSpec((1, 6, d), lambda bb, j: (ctx_row, 0, 0)),
            pl.BlockSpec((1, d), full2),
            pl.BlockSpec((d, IN_PACKED), full2),
            pl.BlockSpec((ROW_TILE, GROUP_WIDTH), lambda bb, j: (j, 0)),
            pl.BlockSpec((ROW_TILE, GROUP_WIDTH), lambda bb, j: (j, 0)),
            pl.BlockSpec((1, GROUP_WIDTH), full2),
            pl.BlockSpec((1, GROUP_WIDTH), full2),
            pl.BlockSpec((1, 128), full2),
            pl.BlockSpec((1, 128), full2),
        ],
        out_specs=[flat(A_W), flat(B_W), flat(G_W), headed_t(4), headed(2), headed_t(2),
                   headed_t(2, pw), headed(2, pw), headed_t(2, pw)],
        out_shape=[fshape(A_W), fshape(B_W), fshape(G_W), hshape_t(4), hshape(2), hshape_t(2),
                   hshape_t(2, pw), hshape(2, pw), hshape_t(2, pw)],
        compiler_params=pltpu.CompilerParams(
            dimension_semantics=("parallel", "arbitrary"), vmem_limit_bytes=VMEM_LIMIT_BYTES),
        name="in_projection",
    )(xcat, mods_l, mods_l, g1, w_packed, cos4, sin4, gq, gqs, gk, gks)


def _hgrn_tile_gates(za_ref, row0, lb, rev, cmat4):
    fcol = 4 * GROUP_WIDTH if rev else 3 * GROUP_WIDTH
    zf = za_ref[0, pl.ds(row0, ROW_TILE), fcol:fcol + GROUP_WIDTH]
    sg = _sigmoid(zf)
    g = jnp.log(jnp.maximum(lb + (1.0 - lb) * sg, FORGET_FLOOR))
    k = (1.0 - lb) * (1.0 - sg)
    return k, _route_l(cmat4, g)


HGRN_SCALES = (1, 2, 4, 8, 16, 32)


def _hgrn_scale_consts(rev):
    t = _iota((CHUNK, GROUP_WIDTH), 0)
    s = _iota((CHUNK, GROUP_WIDTH), 1) & (HEAD_DIM - 1)
    out = []
    for c in HGRN_SCALES:
        sh = c.bit_length()
        t_hi = (t & (2 * c - 1)) >= c
        s_hi = (s & (2 * c - 1)) >= c
        isq = jnp.logical_not(t_hi) if rev else t_hi
        iskey = s_hi if rev else jnp.logical_not(s_hi)
        pair = jnp.where(((t >> sh) == (s >> sh)) & isq & iskey, 1.0, 0.0).astype(F32)
        out.append((isq, pair))
    return out, jnp.where(t == s, 1.0, 0.0).astype(F32)


def _hgrn_boundary_rows(b, c, rev):
    off = c if rev else c - 1
    if 2 * c >= SUB:
        pieces = [jnp.broadcast_to(b[lo + off:lo + off + 1], (2 * c, b.shape[1]))
                  for lo in range(0, CHUNK, 2 * c)]
        return pieces[0] if len(pieces) == 1 else jnp.concatenate(pieces, axis=0)
    pos = _iota(b.shape, 0) & (2 * c - 1)
    out = b
    for j in range(2 * c):
        if j != off:
            out = jnp.where(pos == j, pltpu.roll(b, (j - off) % CHUNK, 0), out)
    return out


def _hgrn_chunk(za_ref, o_ref, st_ref, row0, k, b, rev, want_out, consts):
    ones_bd, bmask, scales, diag = consts
    zq = za_ref[0, pl.ds(row0, CHUNK), 0:GROUP_WIDTH]
    v = za_ref[0, pl.ds(row0, CHUNK), GROUP_WIDTH:2 * GROUP_WIDTH]
    b_end = b[0:1] if rev else b[CHUNK - 1:CHUNK]
    st = st_ref[...]
    vb = v.astype(BF16)

    if want_out:
        q = zq * _sigmoid(zq) * (HEAD_DIM ** -0.5)
        stack = lambda x: jnp.concatenate([x] * GROUP_HEADS, axis=0) * ones_bd
        o = _dot_nt((q * jnp.exp(b)).astype(BF16), st.astype(BF16))
        a = _dot_nt(q.astype(BF16), stack(k.astype(BF16))) * diag
        for c, (isq, pair) in zip(HGRN_SCALES, scales):
            d = b - _hgrn_boundary_rows(b, c, rev)
            x = (jnp.where(isq, q, k) * jnp.exp(jnp.where(isq, d, -d))).astype(BF16)
            a = a + _dot_nt(x, stack(x)) * pair
        o_ref[0, pl.ds(row0, CHUNK), :] += o + _dot(a.astype(BF16), stack(vb))

    kh = k * jnp.exp(b_end - b)
    st_ref[...] = st * jnp.exp(b_end) + _dot_tn(vb, kh.astype(BF16)) * bmask


def _hgrn_kernel(za_ref, lg_ref, gain_ref, o_ref, stf_ref, stb_ref, *, layer, need_ctx):
    t = za_ref.shape[1]
    n_tiles = t // ROW_TILE
    per_tile = ROW_TILE // CHUNK
    lg = lg_ref[...]
    depth = lg.shape[0]
    mx = lg[0]
    for i in range(1, depth):
        mx = jnp.maximum(mx, lg[i])
    es = [jnp.exp(lg[i] - mx) for i in range(depth)]
    tot = es[0]
    for i in range(1, depth):
        tot = tot + es[i]
    sm = [e / tot for e in es]
    cs = sm[0]
    for i in range(1, layer + 1):
        cs = cs + sm[i]
    lbs = cs - sm[0]
    lb_f, lb_b = lbs[0:1], lbs[1:2]

    ones_bd = _head_block_ones(GROUP_WIDTH, GROUP_WIDTH, BF16)
    bmask = _head_block_ones(GROUP_WIDTH, GROUP_WIDTH, F32)
    consts_f = (ones_bd, bmask) + _hgrn_scale_consts(False)
    consts_b = (ones_bd, bmask) + _hgrn_scale_consts(True)
    r4, c4 = _iota((ROW_TILE, ROW_TILE), 0), _iota((ROW_TILE, ROW_TILE), 1)
    same_chunk = (r4 >> 6) == (c4 >> 6)
    cum_f = jnp.where(same_chunk & (c4 <= r4), 1.0, 0.0).astype(BF16)
    cum_b = jnp.where(same_chunk & (c4 >= r4), 1.0, 0.0).astype(BF16)

    stf_ref[...] = jnp.zeros_like(stf_ref)
    stb_ref[...] = jnp.zeros_like(stb_ref)
    o_ref[...] = jnp.zeros_like(o_ref)

    def tile_step(row_f, row_b, want):
        k_f, b_f = _hgrn_tile_gates(za_ref, row_f, lb_f, False, cum_f)
        k_b, b_b = _hgrn_tile_gates(za_ref, row_b, lb_b, True, cum_b)
        for c in range(per_tile):
            lo_f = c * CHUNK
            lo_b = (per_tile - 1 - c) * CHUNK
            _hgrn_chunk(za_ref, o_ref, stf_ref, row_f + lo_f, k_f[lo_f:lo_f + CHUNK], b_f[lo_f:lo_f + CHUNK],
                        False, want, consts_f)
            _hgrn_chunk(za_ref, o_ref, stb_ref, row_b + lo_b, k_b[lo_b:lo_b + CHUNK], b_b[lo_b:lo_b + CHUNK],
                        True, want, consts_b)

    n_ctx_tiles = CTX_LEN // ROW_TILE
    for i in range(n_ctx_tiles):
        tile_step(i * ROW_TILE, (n_ctx_tiles - 1 - i) * ROW_TILE, need_ctx)

    def lat_body(i, carry):
        tile_step(pl.multiple_of((n_ctx_tiles + i) * ROW_TILE, ROW_TILE),
                  pl.multiple_of((n_tiles - 1 - i) * ROW_TILE, ROW_TILE), True)
        return carry

    lax.fori_loop(0, n_tiles - n_ctx_tiles, lat_body, 0)

    gain = gain_ref[...]

    def fin_body(i, carry):
        r0 = pl.multiple_of(i * ROW_TILE, ROW_TILE)
        o = o_ref[0, pl.ds(r0, ROW_TILE), :]
        zo = za_ref[0, pl.ds(r0, ROW_TILE), 2 * GROUP_WIDTH:3 * GROUP_WIDTH]
        o_ref[0, pl.ds(r0, ROW_TILE), :] = _head_rms(o, gain, ones_bd) * (zo * _sigmoid(zo))
        return carry

    lax.fori_loop(0, t // ROW_TILE, fin_body, 0)


def _hgrn(za, lb_logits, gain4, layer, need_ctx):
    b, t, _ = za.shape
    depth = lb_logits.shape[0]
    return pl.pallas_call(
        functools.partial(_hgrn_kernel, layer=layer, need_ctx=need_ctx),
        grid=(b,),
        in_specs=[
            pl.BlockSpec((1, t, A_W), lambda bb: (bb, 0, 0)),
            pl.BlockSpec((depth, 2, GROUP_WIDTH), lambda bb: (0, 0, 0)),
            pl.BlockSpec((1, GROUP_WIDTH), lambda bb: (0, 0)),
        ],
        out_specs=pl.BlockSpec((1, t, GROUP_WIDTH), lambda bb: (bb, 0, 0)),
        out_shape=jax.ShapeDtypeStruct((b, t, GROUP_WIDTH), F32),
        scratch_shapes=[pltpu.VMEM((GROUP_WIDTH, GROUP_WIDTH), F32),
                        pltpu.VMEM((GROUP_WIDTH, GROUP_WIDTH), F32)],
        compiler_params=pltpu.CompilerParams(
            dimension_semantics=("parallel",), vmem_limit_bytes=VMEM_LIMIT_BYTES),
        name="hgrn2_scan",
    )(za, lb_logits, gain4)


def _mlstm_tile_gates(zg_ref, row0, gate_b, consts):
    cmat4, e_i, e_f = consts[0:3]
    hi, mid, lo = _split3(zg_ref[0, pl.ds(row0, ROW_TILE), :] + gate_b)
    ig = _dot(hi, e_i) + _dot(mid, e_i) + _dot(lo, e_i)
    lf = _log_sigmoid(_dot(hi, e_f) + _dot(mid, e_f) + _dot(lo, e_f))
    return ig, _route_l(cmat4, lf)


def _mlstm_chunk(zb_ref, o_ref, ct_ref, n_ref, m_ref, row0, ig, bc, rev, want_out, consts):
    ones_bd, bmask, diag, causal = consts[3:7]
    q = zb_ref[0, pl.ds(row0, CHUNK), 0:GROUP_WIDTH]
    k = zb_ref[0, pl.ds(row0, CHUNK), GROUP_WIDTH:2 * GROUP_WIDTH] * (HEAD_DIM ** -0.5)
    v = zb_ref[0, pl.ds(row0, CHUNK), 2 * GROUP_WIDTH:3 * GROUP_WIDTH]
    b_end = bc[0:1] if rev else bc[CHUNK - 1:CHUNK]
    m_prev = m_ref[0:1]
    n_prev = n_ref[0:1]
    ct = ct_ref[...]
    vb = v.astype(BF16)

    if want_out:
        qb = q.astype(BF16)
        rowv = jnp.sum((ig - bc) * diag, axis=0, keepdims=True)
        dlog = jnp.where(causal, bc + rowv, NEG_BIG)
        segs = []
        for hh in range(GROUP_HEADS):
            mh = jnp.max(dlog[:, hh * HEAD_DIM:(hh + 1) * HEAD_DIM], axis=-1, keepdims=True)
            segs.append(jnp.broadcast_to(mh, (CHUNK, HEAD_DIM)))
        m_loc = jnp.concatenate(segs, axis=-1)
        kk = jnp.concatenate([k.astype(BF16)] * GROUP_HEADS, axis=0) * ones_bd
        vv = jnp.concatenate([vb] * GROUP_HEADS, axis=0) * ones_bd
        pb = (jnp.exp(dlog - m_loc) * _dot_nt(qb, kk)).astype(BF16)
        num_loc = _dot(pb, vv)
        den_loc = _dot(pb, ones_bd)
        qc = _dot_nt(qb, ct.astype(BF16))
        qn = _dot((q * n_prev).astype(BF16), ones_bd)
        a = bc + m_prev
        m_t = jnp.maximum(a, m_loc)
        w_in = jnp.exp(a - m_t)
        w_loc = jnp.exp(m_loc - m_t)
        num = w_in * qc + w_loc * num_loc
        den = w_in * qn + w_loc * den_loc
        o_ref[0, pl.ds(row0, CHUNK), :] += num / jnp.maximum(jnp.abs(den), jnp.exp(-m_t))

    gs = b_end - bc + ig
    g_max = jnp.max(gs, axis=0, keepdims=True)
    kw = k * jnp.exp(gs - g_max)
    m_new = jnp.maximum(b_end + m_prev, g_max)
    w_old = jnp.exp(b_end + m_prev - m_new)
    w_upd = jnp.exp(g_max - m_new)
    ct_ref[...] = ct * w_old + _dot_tn(vb, kw.astype(BF16)) * (bmask * w_upd)
    n_ref[...] = jnp.broadcast_to(n_prev * w_old + jnp.sum(kw, axis=0, keepdims=True) * w_upd, n_ref.shape)
    m_ref[...] = jnp.broadcast_to(m_new, m_ref.shape)


def _mlstm_kernel(zb_ref, zg_ref, gb_ref, gain_ref, o_ref, ctf_ref, ctb_ref, nf_ref, nb_ref,
                  mf_ref, mb_ref, *, need_ctx):
    t = zb_ref.shape[1]
    n_tiles = t // ROW_TILE
    per_tile = ROW_TILE // CHUNK
    ones_bd = _head_block_ones(GROUP_WIDTH, GROUP_WIDTH, BF16)
    bmask = _head_block_ones(GROUP_WIDTH, GROUP_WIDTH, F32)
    gcol = _iota((128, GROUP_WIDTH), 0)
    ghead = _iota((128, GROUP_WIDTH), 1) >> 6
    route = lambda base: jnp.where(gcol == base + ghead, 1.0, 0.0).astype(BF16)
    tt = _iota((CHUNK, GROUP_WIDTH), 0)
    ss = _iota((CHUNK, GROUP_WIDTH), 1) & (HEAD_DIM - 1)
    diag = jnp.where(tt == ss, 1.0, 0.0).astype(F32)
    r4, c4 = _iota((ROW_TILE, ROW_TILE), 0), _iota((ROW_TILE, ROW_TILE), 1)
    same_chunk = (r4 >> 6) == (c4 >> 6)
    cum4 = lambda rev: jnp.where(same_chunk & ((c4 >= r4) if rev else (c4 <= r4)), 1.0, 0.0).astype(BF16)
    cf = (cum4(False), route(0), route(2 * GROUP_HEADS), ones_bd, bmask, diag, ss <= tt)
    cb = (cum4(True), route(GROUP_HEADS), route(3 * GROUP_HEADS), ones_bd, bmask, diag, ss >= tt)
    gate_b = gb_ref[...]

    for r in (ctf_ref, ctb_ref, nf_ref, nb_ref, mf_ref, mb_ref, o_ref):
        r[...] = jnp.zeros_like(r)

    def tile_step(row_f, row_b, want):
        ig_f, bc_f = _mlstm_tile_gates(zg_ref, row_f, gate_b, cf)
        ig_b, bc_b = _mlstm_tile_gates(zg_ref, row_b, gate_b, cb)
        for c in range(per_tile):
            lo_f = c * CHUNK
            lo_b = (per_tile - 1 - c) * CHUNK
            _mlstm_chunk(zb_ref, o_ref, ctf_ref, nf_ref, mf_ref, row_f + lo_f,
                         ig_f[lo_f:lo_f + CHUNK], bc_f[lo_f:lo_f + CHUNK], False, want, cf)
            _mlstm_chunk(zb_ref, o_ref, ctb_ref, nb_ref, mb_ref, row_b + lo_b,
                         ig_b[lo_b:lo_b + CHUNK], bc_b[lo_b:lo_b + CHUNK], True, want, cb)

    n_ctx_tiles = CTX_LEN // ROW_TILE
    for i in range(n_ctx_tiles):
        tile_step(i * ROW_TILE, (n_ctx_tiles - 1 - i) * ROW_TILE, need_ctx)

    def lat_body(i, carry):
        tile_step(pl.multiple_of((n_ctx_tiles + i) * ROW_TILE, ROW_TILE),
                  pl.multiple_of((n_tiles - 1 - i) * ROW_TILE, ROW_TILE), True)
        return carry

    lax.fori_loop(0, n_tiles - n_ctx_tiles, lat_body, 0)

    gain = gain_ref[...]

    def fin_body(i, carry):
        r0 = pl.multiple_of(i * ROW_TILE, ROW_TILE)
        o = o_ref[0, pl.ds(r0, ROW_TILE), :]
        zo = zb_ref[0, pl.ds(r0, ROW_TILE), 3 * GROUP_WIDTH:4 * GROUP_WIDTH]
        o_ref[0, pl.ds(r0, ROW_TILE), :] = _head_rms(o, gain, ones_bd) * _sigmoid(zo)
        return carry

    lax.fori_loop(0, t // ROW_TILE, fin_body, 0)


def _mlstm(zb, zg, gate_b128, gain4, need_ctx):
    b, t, _ = zb.shape
    return pl.pallas_call(
        functools.partial(_mlstm_kernel, need_ctx=need_ctx),
        grid=(b,),
        in_specs=[
            pl.BlockSpec((1, t, B_W), lambda bb: (bb, 0, 0)),
            pl.BlockSpec((1, t, G_W), lambda bb: (bb, 0, 0)),
            pl.BlockSpec((1, G_W), lambda bb: (0, 0)),
            pl.BlockSpec((1, GROUP_WIDTH), lambda bb: (0, 0)),
        ],
        out_specs=pl.BlockSpec((1, t, GROUP_WIDTH), lambda bb: (bb, 0, 0)),
        out_shape=jax.ShapeDtypeStruct((b, t, GROUP_WIDTH), F32),
        scratch_shapes=[pltpu.VMEM((GROUP_WIDTH, GROUP_WIDTH), F32),
                        pltpu.VMEM((GROUP_WIDTH, GROUP_WIDTH), F32),
                        pltpu.VMEM((SUB, GROUP_WIDTH), F32), pltpu.VMEM((SUB, GROUP_WIDTH), F32),
                        pltpu.VMEM((SUB, GROUP_WIDTH), F32), pltpu.VMEM((SUB, GROUP_WIDTH), F32)],
        compiler_params=pltpu.CompilerParams(
            dimension_semantics=("parallel",), vmem_limit_bytes=VMEM_LIMIT_BYTES),
        name="mlstm_scan",
    )(zb, zg, gate_b128, gain4)


def _attend_t(qt, pieces):
    s_list = []
    for k, _, bias in pieces:
        s = _dot(k, qt)
        s_list.append(s if bias is None else s + bias)
    m = jnp.max(s_list[0], axis=0, keepdims=True)
    for s in s_list[1:]:
        m = jnp.maximum(m, jnp.max(s, axis=0, keepdims=True))
    l = acc = None
    for s, (_, vt, _) in zip(s_list, pieces):
        p = jnp.exp2(s - m)
        ls = jnp.sum(p, axis=0, keepdims=True)
        pv = _dot(vt, p.astype(BF16))
        l = ls if l is None else l + ls
        acc = pv if acc is None else acc + pv
    return acc / l


def _gqa_kernel(q_ref, k_ref, vt_ref, o_ref, *, need_ctx):
    j = pl.program_id(2)
    t = k_ref.shape[2]
    g = q_ref.shape[1]

    def attend(nk):
        qt = jnp.concatenate([q_ref[0, i] for i in range(g)], axis=1)
        ot = _attend_t(qt, [(k_ref[0, 0, 0:nk, :], vt_ref[0, 0, :, 0:nk], None)])
        o_ref[0] = jnp.concatenate([ot[:, i * ROW_TILE:(i + 1) * ROW_TILE] for i in range(g)], axis=0).T

    @pl.when(j == 0)
    def _():
        if need_ctx:
            attend(CTX_LEN)
        else:
            o_ref[...] = jnp.zeros_like(o_ref)

    @pl.when(j > 0)
    def _():
        attend(t)


def _gqa(qct, kc, vct, need_ctx):
    b, _, t, _ = kc.shape
    nq = t // ROW_TILE
    g = GROUP_HEADS // KV_HEADS
    return pl.pallas_call(
        functools.partial(_gqa_kernel, need_ctx=need_ctx),
        grid=(b, KV_HEADS, nq),
        in_specs=[
            pl.BlockSpec((1, g, HEAD_DIM, ROW_TILE), lambda bb, h, j: (bb, h, 0, j)),
            pl.BlockSpec((1, 1, t, HEAD_DIM), lambda bb, h, j: (bb, h, 0, 0)),
            pl.BlockSpec((1, 1, HEAD_DIM, t), lambda bb, h, j: (bb, h, 0, 0)),
        ],
        out_specs=pl.BlockSpec((1, ROW_TILE, g * HEAD_DIM), lambda bb, h, j: (bb, j, h)),
        out_shape=jax.ShapeDtypeStruct((b, t, GROUP_WIDTH), F32),
        compiler_params=pltpu.CompilerParams(
            dimension_semantics=("parallel", "parallel", "arbitrary"), vmem_limit_bytes=VMEM_LIMIT_BYTES),
        name="gqa_attention",
    )(qct, kc, vct)


def _na_plan(rows):
    kh = min(NA_KH, rows)
    win = min(kh + NA_QROWS, rows)
    assert rows % NA_QROWS == 0 and (win * GRID_W) % 128 == 0
    blocks, variants = [], []
    for i in range(rows // NA_QROWS):
        qrows = np.arange(NA_QROWS * i, NA_QROWS * (i + 1))
        r0 = np.clip(qrows - kh // 2, 0, rows - kh)
        start = int(np.clip(NA_QROWS * i - kh // 2, 0, rows - win))
        assert start % 2 == 0 and start <= r0.min() and r0.max() + kh <= start + win
        key = (start - NA_QROWS * i, tuple(int(v) for v in r0 - qrows))
        if key not in variants:
            variants.append(key)
        blocks.append((start, variants.index(key)))
    return kh, win, tuple(blocks), variants


def _na_kernel(q_ref, k_ref, vt_ref, bias_ref, o_ref, *, need_ctx, blocks):
    nq = NA_QROWS * GRID_W
    nk = bias_ref.shape[2]
    kctx = k_ref[0, 0, 0:CTX_LEN, :]
    vtctx = vt_ref[0, 0, :, 0:CTX_LEN]

    def paired_queries(lo, n):
        q = q_ref[0, 0, :, lo:lo + n]
        z = jnp.zeros((HEAD_DIM, n), q.dtype)
        return jnp.concatenate([jnp.concatenate([q[0:HEAD_DIM], z], axis=1),
                                jnp.concatenate([z, q[HEAD_DIM:2 * HEAD_DIM]], axis=1)], axis=0)

    def unpair(ot, n):
        return jnp.concatenate([ot[0:HEAD_DIM, 0:n], ot[HEAD_DIM:2 * HEAD_DIM, n:2 * n]], axis=0).T

    if need_ctx:
        ot = _attend_t(paired_queries(0, CTX_LEN), [(kctx, vtctx, None)])
        o_ref[0, 0:CTX_LEN, :] = unpair(ot, CTX_LEN)
    else:
        o_ref[0, 0:CTX_LEN, :] = jnp.zeros((CTX_LEN, 2 * HEAD_DIM), F32)

    for i, (start, var) in enumerate(blocks):
        q0 = CTX_LEN + i * nq
        k0 = CTX_LEN + start * GRID_W
        pieces = [(kctx, vtctx, None),
                  (k_ref[0, 0, k0:k0 + nk, :], vt_ref[0, 0, :, k0:k0 + nk], bias_ref[0, var])]
        o_ref[0, q0:q0 + nq, :] = unpair(_attend_t(paired_queries(q0, nq), pieces), nq)


def _na(qdt, kd, vdt, bias_t, blocks, need_ctx):
    b, npair, t, pw = kd.shape
    nvar, nk, nq2 = bias_t.shape[1:]
    blk = pl.BlockSpec((1, 1, t, pw), lambda h, bb: (bb, h, 0, 0))
    blk_t = pl.BlockSpec((1, 1, pw, t), lambda h, bb: (bb, h, 0, 0))
    return pl.pallas_call(
        functools.partial(_na_kernel, need_ctx=need_ctx, blocks=blocks),
        grid=(npair, b),
        in_specs=[blk_t, blk, blk_t,
                  pl.BlockSpec((1, nvar, nk, nq2), lambda h, bb: (h, 0, 0, 0))],
        out_specs=pl.BlockSpec((1, t, pw), lambda h, bb: (bb, 0, h)),
        out_shape=jax.ShapeDtypeStruct((b, t, GROUP_WIDTH), F32),
        compiler_params=pltpu.CompilerParams(
            dimension_semantics=("parallel", "arbitrary"), vmem_limit_bytes=VMEM_LIMIT_BYTES),
        name="neighbourhood_attention",
    )(qdt, kd, vdt, bias_t)


def _na_bias_table(rpb, kh, win, variants):
    rows_q = NA_QROWS
    nrow, ncol = 2 * NA_KH - 1, 2 * NA_KW - 1
    col = np.arange(GRID_W)
    cstart = np.clip(col - NA_KW // 2, 0, GRID_W - NA_KW)
    col_in = (col[None, :] >= cstart[:, None]) & (col[None, :] < cstart[:, None] + NA_KW)
    col_idx = np.clip(col[None, :] - col[:, None], 1 - NA_KW, NA_KW - 1) + NA_KW - 1
    col_sel = (col_idx.T[None] == np.arange(ncol)[:, None, None]).astype(np.float32)
    row_sel, valid = [], []
    for ds, offs in variants:
        rel = ds + np.arange(win)[:, None] - np.arange(rows_q)[None, :]
        offs = np.asarray(offs)[None, :]
        valid.append((rel >= offs) & (rel < offs + kh))
        row_sel.append((rel + NA_KH - 1)[..., None] == np.arange(nrow))
    row_sel = np.stack(row_sel).astype(np.float32)
    keep = np.stack(valid)[:, :, None, None, :, None] & col_in.T[None, None, :, None, None, :]
    pairs = rpb.reshape(rpb.shape[0] // 2, 2, nrow, ncol).astype(F32)
    tbl = jnp.einsum('vxar,pgrc,cyq->pvxygaq', row_sel, pairs, col_sel, precision=lax.Precision.HIGHEST)
    tbl = jnp.where(keep[None], tbl * LOG2E, NEG_BIG)
    return tbl.reshape(rpb.shape[0] // 2, len(variants), win * GRID_W, 2 * rows_q * GRID_W)


def _outmlp_kernel(x_ref, a_ref, b_ref, c_ref, d_ref, mlat_ref, mctx_ref, g2_ref, wo_ref, w1_ref,
                   w2_ref, gf_ref, o_ref, *, tile_off, final):
    j = pl.program_id(1) + tile_off
    mod = jnp.where(j == 0, mctx_ref[0], mlat_ref[0])
    cat = jnp.concatenate([a_ref[0], b_ref[0], c_ref[0], d_ref[0]], axis=-1).astype(BF16)
    x1 = x_ref[0] + mod[2:3] * _dot(cat, wo_ref[...])
    ms = jnp.mean(x1 * x1, axis=-1, keepdims=True)
    h2 = ((x1 * lax.rsqrt(ms + EPS) * g2_ref[...]) * (1.0 + mod[4:5]) + mod[3:4]).astype(BF16)
    hid = w1_ref.shape[1]
    step = 1024
    acc = jnp.zeros(x1.shape, F32)
    for c in range(hid // step):
        u = jnp.maximum(_dot(h2, w1_ref[:, c * step:(c + 1) * step]), 0.0)
        acc = acc + _dot((u * u).astype(BF16), w2_ref[c * step:(c + 1) * step, :])
    x2 = x1 + mod[5:6] * acc
    if final:
        ms2 = jnp.mean(x2 * x2, axis=-1, keepdims=True)
        x2 = x2 * lax.rsqrt(ms2 + EPS) * gf_ref[...]
    o_ref[0] = x2


def _outmlp(xcat, a, bm, c, dd, mods_l, g2, wo, w1, w2, gf, final):
    b, t, d = xcat.shape
    tile_off = CTX_LEN // ROW_TILE if final else 0
    nt = t // ROW_TILE - tile_off
    ctx_row = mods_l.shape[0] - 1
    full2 = lambda bb, j: (0, 0)
    grp = pl.BlockSpec((1, ROW_TILE, GROUP_WIDTH), lambda bb, j: (bb, j + tile_off, 0))
    return pl.pallas_call(
        functools.partial(_outmlp_kernel, tile_off=tile_off, final=final),
        grid=(b, nt),
        in_specs=[
            pl.BlockSpec((1, ROW_TILE, d), lambda bb, j: (bb, j + tile_off, 0)),
            grp, grp, grp, grp,
            pl.BlockSpec((1, 6, d), lambda bb, j: (bb, 0, 0)),
            pl.BlockSpec((1, 6, d), lambda bb, j: (ctx_row, 0, 0)),
            pl.BlockSpec((1, d), full2),
            pl.BlockSpec(wo.shape, full2),
            pl.BlockSpec(w1.shape, full2),
            pl.BlockSpec(w2.shape, full2),
            pl.BlockSpec((1, d), full2),
        ],
        out_specs=pl.BlockSpec((1, ROW_TILE, d), lambda bb, j: (bb, j, 0)),
        out_shape=jax.ShapeDtypeStruct((b, nt * ROW_TILE, d), F32),
        compiler_params=pltpu.CompilerParams(
            dimension_semantics=("parallel", "arbitrary"), vmem_limit_bytes=VMEM_LIMIT_BYTES),
        name="out_projection_mlp",
    )(xcat, a, bm, c, dd, mods_l, mods_l, g2, wo, w1, w2, gf)


def _pack_in_projection(w):
    gw, kvw = GROUP_WIDTH, KV_HEADS * HEAD_DIM
    g0 = 9 * gw
    c0 = g0 + 4 * GROUP_HEADS
    d0 = c0 + gw + 2 * kvw

    def pair_swapped(cols):
        n = cols.shape[1]
        return cols.reshape(cols.shape[0], n // 2, 2)[:, :, ::-1].reshape(cols.shape[0], n)

    q, k = w[:, c0:c0 + gw], w[:, c0 + gw:c0 + gw + kvw]
    parts = [w[:, 0:g0], w[:, g0:c0], jnp.zeros((w.shape[0], G_W - 4 * GROUP_HEADS), w.dtype),
             q, pair_swapped(q), k, pair_swapped(k), w[:, c0 + gw + kvw:d0], w[:, d0:d0 + 3 * gw]]
    packed = jnp.concatenate(parts, axis=1).astype(BF16)
    assert packed.shape[1] == IN_PACKED
    return packed


def _rope_tables(n_tokens):
    t = np.arange(n_tokens)
    row = (t // GRID_W).astype(np.float32)
    col = (t % GRID_W).astype(np.float32)
    axis_dims = HEAD_DIM // 2
    inv = np.power(np.float32(ROPE_THETA),
                   (-2.0 * np.arange(axis_dims // 2, dtype=np.float32) / axis_dims).astype(np.float32))
    ang = np.concatenate([row[:, None] * inv, col[:, None] * inv], axis=-1).astype(np.float32)
    cos = np.repeat(np.cos(ang).astype(np.float32), 2, axis=-1)
    sign = np.where(np.arange(HEAD_DIM) % 2 == 0, -1.0, 1.0).astype(np.float32)
    sin = np.repeat(np.sin(ang).astype(np.float32), 2, axis=-1) * sign
    cos = np.concatenate([np.ones((CTX_LEN, HEAD_DIM), np.float32), cos], axis=0)
    sin = np.concatenate([np.zeros((CTX_LEN, HEAD_DIM), np.float32), sin], axis=0)
    return jnp.asarray(np.tile(cos, (1, GROUP_HEADS))), jnp.asarray(np.tile(sin, (1, GROUP_HEADS)))


def kernel(x, c, ctx, c_ctx, w_mod, b_mod, norm1_g, norm2_g, w_in, hgrn_lb_logits, hgrn_norm_g,
           mlstm_gate_b, mlstm_norm_g, gqa_qnorm_g, gqa_knorm_g, na_rpb, w_out, w_mlp1, w_mlp2,
           final_norm_g):
    bsz, seq, d = x.shape
    depth = w_mod.shape[0]
    rows = seq // GRID_W

    n_c = bsz + 1
    pad = (-n_c) % 8
    cc = jnp.concatenate([c, c_ctx[None, :], jnp.zeros((pad, d), F32)], axis=0)
    mods = _modulation(cc, w_mod, b_mod)[:, :n_c].reshape(depth, n_c, 6, d)

    cos4, sin4 = _rope_tables(seq)
    swap = np.arange(HEAD_DIM) ^ 1

    xcat = jnp.concatenate([ctx, x], axis=1)
    out = None
    for l in range(depth):
        need_ctx = l < depth - 1
        w_packed = _pack_in_projection(w_in[l])
        gq = jnp.tile(gqa_qnorm_g[l], GROUP_HEADS)[None, :]
        gqs = jnp.tile(gqa_qnorm_g[l][swap], GROUP_HEADS)[None, :]
        gk = jnp.tile(gqa_knorm_g[l], KV_HEADS)[None, :]
        gks = jnp.tile(gqa_knorm_g[l][swap], KV_HEADS)[None, :]
        za, zb, zg, qc, kc, vc, qd, kd, vd = _inproj(
            xcat, mods[l], norm1_g[l][None, :], w_packed, cos4, sin4, gq, gqs, gk, gks)

        a = _hgrn(za, hgrn_lb_logits, jnp.tile(hgrn_norm_g[l], GROUP_HEADS)[None, :], l, need_ctx)
        gate_b = jnp.concatenate([mlstm_gate_b[l], jnp.zeros((G_W - 4 * GROUP_HEADS,), F32)])[None, :]
        bm = _mlstm(zb, zg, gate_b, jnp.tile(mlstm_norm_g[l], GROUP_HEADS)[None, :], need_ctx)
        cg = _gqa(qc, kc, vc, need_ctx)
        kh, win, blocks, variants = _na_plan(rows)
        dg = _na(qd, kd, vd, _na_bias_table(na_rpb[l], kh, win, variants), blocks, need_ctx)

        res = _outmlp(xcat, a, bm, cg, dg, mods[l], norm2_g[l][None, :], w_out[l].astype(BF16),
                      w_mlp1[l].astype(BF16), w_mlp2[l].astype(BF16), final_norm_g[None, :],
                      final=not need_ctx)
        if need_ctx:
            xcat = res
        else:
            out = res
    return out
```

```python
import functools

import numpy as np
import jax
import jax.numpy as jnp
from jax import lax
from jax.experimental import pallas as pl
from jax.experimental.pallas import tpu as pltpu

F32 = jnp.float32
BF16 = jnp.bfloat16

D_MODEL = 1024
GRID_W = 64
CTX_LEN = 256
HEAD_DIM = 64
GROUP_HEADS = 4
GROUP_WIDTH = GROUP_HEADS * HEAD_DIM
KV_HEADS = 2
MLP_HIDDEN = 4 * D_MODEL
CHUNK = 64
SUB = 8
NA_KH = 8
NA_KW = 16
ROPE_THETA = 10000.0
EPS = 1e-6
NEG_BIG = -1e30
FORGET_FLOOR = 1e-20
ROW_TILE = 256
VMEM_LIMIT_BYTES = 56 * 1024 * 1024
LOG2E = 1.4426950408889634
Q_SCALE = HEAD_DIM ** -0.5 * LOG2E
NA_QROWS = 4
NA_AHEAD = 2
GQA_Q_TILES = 3

A_OFF, A_W = 0, 5 * GROUP_WIDTH
B_OFF, B_W = A_OFF + A_W, 4 * GROUP_WIDTH
G_OFF, G_W = B_OFF + B_W, 128
C_OFF, C_W = G_OFF + G_W, 896
D_OFF, D_W = C_OFF + C_W, 3 * GROUP_WIDTH
IN_PACKED = D_OFF + D_W


def _dot(a, b):
    return jnp.dot(a, b, preferred_element_type=F32)


def _dot_nt(a, b):
    return lax.dot_general(a, b, (((1,), (1,)), ((), ())), preferred_element_type=F32)


def _dot_tn(a, b):
    return lax.dot_general(a, b, (((0,), (0,)), ((), ())), preferred_element_type=F32)


def _split3(x):
    hi = x.astype(BF16)
    r = x - hi.astype(F32)
    mid = r.astype(BF16)
    lo = (r - mid.astype(F32)).astype(BF16)
    return hi, mid, lo


def _route_l(m01, x):
    hi, mid, lo = _split3(x)
    return _dot(m01, hi) + _dot(m01, mid) + _dot(m01, lo)


def _route_r(x, m01):
    hi, mid, lo = _split3(x)
    return _dot(hi, m01) + _dot(mid, m01) + _dot(lo, m01)


def _sigmoid(x):
    return 1.0 / (1.0 + jnp.exp(-x))


def _log_sigmoid(x):
    return -(jnp.maximum(-x, 0.0) + jnp.log1p(jnp.exp(-jnp.abs(x))))


def _iota(shape, dim):
    return lax.broadcasted_iota(jnp.int32, shape, dim)


def _head_block_ones(n, m, dtype):
    return jnp.where((_iota((n, m), 0) >> 6) == (_iota((n, m), 1) >> 6), 1.0, 0.0).astype(dtype)


def _cumsum_matrix(rev):
    r, c = _iota((CHUNK, CHUNK), 0), _iota((CHUNK, CHUNK), 1)
    keep = (c >= r) if rev else (c <= r)
    return jnp.where(keep, 1.0, 0.0).astype(BF16)


def _head_rms(o, gain, ones_bd):
    ssq = _route_r(o * o, ones_bd)
    return o * lax.rsqrt(ssq * (1.0 / HEAD_DIM) + EPS) * gain


def _mod_kernel(c_ref, w_ref, b_ref, o_ref):
    c = c_ref[...]
    s = c * _sigmoid(c)
    o_ref[0] = _dot(s.astype(BF16), w_ref[0].astype(BF16)) + b_ref[0]


def _modulation(cc, w_mod, b_mod):
    depth, d, n = w_mod.shape
    rows = cc.shape[0]
    tn = 1024
    return pl.pallas_call(
        _mod_kernel,
        grid=(depth, n // tn),
        in_specs=[
            pl.BlockSpec((rows, d), lambda l, j: (0, 0)),
            pl.BlockSpec((1, d, tn), lambda l, j: (l, 0, j)),
            pl.BlockSpec((1, 1, tn), lambda l, j: (l, 0, j)),
        ],
        out_specs=pl.BlockSpec((1, rows, tn), lambda l, j: (l, 0, j)),
        out_shape=jax.ShapeDtypeStruct((depth, rows, n), F32),
        compiler_params=pltpu.CompilerParams(
            dimension_semantics=("arbitrary", "arbitrary"), vmem_limit_bytes=VMEM_LIMIT_BYTES),
        name="adaln_modulation",
    )(cc, w_mod, b_mod.reshape(depth, 1, n))


def _inproj_kernel(x_ref, mlat_ref, mctx_ref, g1_ref, w_ref, cos_ref, sin_ref, gq_ref, gqs_ref,
                   gk_ref, gks_ref, za_ref, zb_ref, zg_ref, qc_ref, kc_ref, vc_ref,
                   qd_ref, kd_ref, vd_ref):
    j = pl.program_id(1)
    mod = jnp.where(j == 0, mctx_ref[0], mlat_ref[0])
    x = x_ref[0]
    ms = jnp.mean(x * x, axis=-1, keepdims=True)
    y = x * lax.rsqrt(ms + EPS) * g1_ref[...]
    h = (y * (1.0 + mod[1:2]) + mod[0:1]).astype(BF16)

    zc = _dot(h, w_ref[:, C_OFF:C_OFF + C_W])
    zd = _dot(h, w_ref[:, D_OFF:D_OFF + D_W])
    za_ref[0] = _dot(h, w_ref[:, A_OFF:A_OFF + A_W])

    ones4 = _head_block_ones(GROUP_WIDTH, GROUP_WIDTH, BF16)
    cos, sin = cos_ref[...], sin_ref[...]
    q, qs = zc[:, 0:256], zc[:, 256:512]
    k, ks = zc[:, 512:640], zc[:, 640:768]
    ssq_q = _route_r(q * q, ones4)
    ssq_k = _route_r(k * k, ones4[0:128, 0:128])
    zb_ref[0] = _dot(h, w_ref[:, B_OFF:B_OFF + B_W])
    zg_ref[0] = _dot(h, w_ref[:, G_OFF:G_OFF + G_W])
    rq = lax.rsqrt(ssq_q * (1.0 / HEAD_DIM) + EPS)
    qrt = ((rq * (q * gq_ref[...] * cos + qs * gqs_ref[...] * sin)) * Q_SCALE).T
    rk = lax.rsqrt(ssq_k * (1.0 / HEAD_DIM) + EPS)
    kr = (rk * (k * gk_ref[...] * cos[:, 0:128] + ks * gks_ref[...] * sin[:, 0:128])).astype(BF16)
    vvt = zc[:, 768:896].T
    for hh in range(GROUP_HEADS):
        qc_ref[0, hh] = qrt[hh * HEAD_DIM:(hh + 1) * HEAD_DIM].astype(BF16)
    for hh in range(KV_HEADS):
        kc_ref[0, hh] = kr[:, hh * HEAD_DIM:(hh + 1) * HEAD_DIM]
        vc_ref[0, hh] = vvt[hh * HEAD_DIM:(hh + 1) * HEAD_DIM].astype(BF16)

    qdt = (zd[:, 0:256] * Q_SCALE).T
    kdb = zd[:, 256:512].astype(BF16)
    vdt = zd[:, 512:768].T
    for pp in range(GROUP_HEADS // 2):
        lo, hi = pp * 2 * HEAD_DIM, (pp + 1) * 2 * HEAD_DIM
        qd_ref[0, pp] = qdt[lo:hi].astype(BF16)
        kd_ref[0, pp] = kdb[:, lo:hi]
        vd_ref[0, pp] = vdt[lo:hi].astype(BF16)


def _inproj(xcat, mods_l, g1, w_packed, cos4, sin4, gq, gqs, gk, gks):
    b, t, d = xcat.shape
    nt = t // ROW_TILE
    ctx_row = mods_l.shape[0] - 1
    full2 = lambda bb, j: (0, 0)
    headed = lambda nh, w=HEAD_DIM: pl.BlockSpec((1, nh, ROW_TILE, w), lambda bb, j: (bb, 0, j, 0))
    headed_t = lambda nh, w=HEAD_DIM: pl.BlockSpec((1, nh, w, ROW_TILE), lambda bb, j: (bb, 0, 0, j))
    flat = lambda w: pl.BlockSpec((1, ROW_TILE, w), lambda bb, j: (bb, j, 0))
    hshape = lambda nh, w=HEAD_DIM: jax.ShapeDtypeStruct((b, nh, t, w), BF16)
    hshape_t = lambda nh, w=HEAD_DIM: jax.ShapeDtypeStruct((b, nh, w, t), BF16)
    fshape = lambda w: jax.ShapeDtypeStruct((b, t, w), F32)
    pw = 2 * HEAD_DIM
    return pl.pallas_call(
        _inproj_kernel,
        grid=(b, nt),
        in_specs=[
            pl.BlockSpec((1, ROW_TILE, d), lambda bb, j: (bb, j, 0)),
            pl.BlockSpec((1, 6, d), lambda bb, j: (bb, 0, 0)),
            pl.BlockSpec((1, 6, d), lambda bb, j: (ctx_row, 0, 0)),
            pl.BlockSpec((1, d), full2),
            pl.BlockSpec((d, IN_PACKED), full2),
            pl.BlockSpec((ROW_TILE, GROUP_WIDTH), lambda bb, j: (j, 0)),
            pl.BlockSpec((ROW_TILE, GROUP_WIDTH), lambda bb, j: (j, 0)),
            pl.BlockSpec((1, GROUP_WIDTH), full2),
            pl.BlockSpec((1, GROUP_WIDTH), full2),
            pl.BlockSpec((1, 128), full2),
            pl.BlockSpec((1, 128), full2),
        ],
        out_specs=[flat(A_W), flat(B_W), flat(G_W), headed_t(4), headed(2), headed_t(2),
                   headed_t(2, pw), headed(2, pw), headed_t(2, pw)],
        out_shape=[fshape(A_W), fshape(B_W), fshape(G_W), hshape_t(4), hshape(2), hshape_t(2),
                   hshape_t(2, pw), hshape(2, pw), hshape_t(2, pw)],
        compiler_params=pltpu.CompilerParams(
            dimension_semantics=("parallel", "arbitrary"), vmem_limit_bytes=VMEM_LIMIT_BYTES),
        name="in_projection",
    )(xcat, mods_l, mods_l, g1, w_packed, cos4, sin4, gq, gqs, gk, gks)


def _hgrn_tile_gates(za_ref, row0, lb, rev, cmat4):
    fcol = 4 * GROUP_WIDTH if rev else 3 * GROUP_WIDTH
    zf = za_ref[0, pl.ds(row0, ROW_TILE), fcol:fcol + GROUP_WIDTH]
    sg = _sigmoid(zf)
    g = jnp.log(jnp.maximum(lb + (1.0 - lb) * sg, FORGET_FLOOR))
    k = (1.0 - lb) * (1.0 - sg)
    return k, _route_l(cmat4, g)


HGRN_SCALES = (1, 2, 4, 8, 16, 32)


def _hgrn_scale_consts(rev):
    t = _iota((CHUNK, GROUP_WIDTH), 0)
    s = _iota((CHUNK, GROUP_WIDTH), 1) & (HEAD_DIM - 1)
    out = []
    for c in HGRN_SCALES:
        sh = c.bit_length()
        t_hi = (t & (2 * c - 1)) >= c
        s_hi = (s & (2 * c - 1)) >= c
        isq = jnp.logical_not(t_hi) if rev else t_hi
        iskey = s_hi if rev else jnp.logical_not(s_hi)
        pair = jnp.where(((t >> sh) == (s >> sh)) & isq & iskey, 1.0, 0.0).astype(F32)
        out.append((isq, pair))
    return out, jnp.where(t == s, 1.0, 0.0).astype(F32)


def _hgrn_boundary_rows(b, c, rev):
    off = c if rev else c - 1
    if 2 * c >= SUB:
        pieces = [jnp.broadcast_to(b[lo + off:lo + off + 1], (2 * c, b.shape[1]))
                  for lo in range(0, CHUNK, 2 * c)]
        return pieces[0] if len(pieces) == 1 else jnp.concatenate(pieces, axis=0)
    pos = _iota(b.shape, 0) & (2 * c - 1)
    out = b
    for j in range(2 * c):
        if j != off:
            out = jnp.where(pos == j, pltpu.roll(b, (j - off) % CHUNK, 0), out)
    return out


def _hgrn_chunk_local(za_ref, row0, k, b, rev, want_out, consts):
    ones_bd, bmask, scales, diag = consts
    v = za_ref[0, pl.ds(row0, CHUNK), GROUP_WIDTH:2 * GROUP_WIDTH]
    b_end = b[0:1] if rev else b[CHUNK - 1:CHUNK]
    vb = v.astype(BF16)
    upd = _dot_tn(vb, (k * jnp.exp(b_end - b)).astype(BF16)) * bmask
    if not want_out:
        return jnp.exp(b_end), upd, None, None
    zq = za_ref[0, pl.ds(row0, CHUNK), 0:GROUP_WIDTH]
    q = zq * _sigmoid(zq) * (HEAD_DIM ** -0.5)
    stack = lambda x: jnp.concatenate([x] * GROUP_HEADS, axis=0) * ones_bd
    a = _dot_nt(q.astype(BF16), stack(k.astype(BF16))) * diag
    for c, (isq, pair) in zip(HGRN_SCALES, scales):
        d = b - _hgrn_boundary_rows(b, c, rev)
        x = (jnp.where(isq, q, k) * jnp.exp(jnp.where(isq, d, -d))).astype(BF16)
        a = a + _dot_nt(x, stack(x)) * pair
    return jnp.exp(b_end), upd, (q * jnp.exp(b)).astype(BF16), _dot(a.astype(BF16), stack(vb))


def _hgrn_chunk_carry(o_ref, row0, st, local):
    decay, upd, qe, o_intra = local
    if qe is not None:
        o_ref[0, pl.ds(row0, CHUNK), :] += o_intra + _dot_nt(qe, st.astype(BF16))
    return st * decay + upd


def _hgrn_kernel(za_ref, lg_ref, gain_ref, o_ref, stf_ref, stb_ref, *, layer, need_ctx):
    t = za_ref.shape[1]
    n_tiles = t // ROW_TILE
    per_tile = ROW_TILE // CHUNK
    lg = lg_ref[...]
    depth = lg.shape[0]
    mx = lg[0]
    for i in range(1, depth):
        mx = jnp.maximum(mx, lg[i])
    es = [jnp.exp(lg[i] - mx) for i in range(depth)]
    tot = es[0]
    for i in range(1, depth):
        tot = tot + es[i]
    sm = [e / tot for e in es]
    cs = sm[0]
    for i in range(1, layer + 1):
        cs = cs + sm[i]
    lbs = cs - sm[0]
    lb_f, lb_b = lbs[0:1], lbs[1:2]

    ones_bd = _head_block_ones(GROUP_WIDTH, GROUP_WIDTH, BF16)
    bmask = _head_block_ones(GROUP_WIDTH, GROUP_WIDTH, F32)
    consts_f = (ones_bd, bmask) + _hgrn_scale_consts(False)
    consts_b = (ones_bd, bmask) + _hgrn_scale_consts(True)
    r4, c4 = _iota((ROW_TILE, ROW_TILE), 0), _iota((ROW_TILE, ROW_TILE), 1)
    same_chunk = (r4 >> 6) == (c4 >> 6)
    cum_f = jnp.where(same_chunk & (c4 <= r4), 1.0, 0.0).astype(BF16)
    cum_b = jnp.where(same_chunk & (c4 >= r4), 1.0, 0.0).astype(BF16)

    stf_ref[...] = jnp.zeros_like(stf_ref)
    stb_ref[...] = jnp.zeros_like(stb_ref)
    o_ref[...] = jnp.zeros_like(o_ref)

    def tile_step(row_f, row_b, want):
        k_f, b_f = _hgrn_tile_gates(za_ref, row_f, lb_f, False, cum_f)
        k_b, b_b = _hgrn_tile_gates(za_ref, row_b, lb_b, True, cum_b)
        lo_f = [c * CHUNK for c in range(per_tile)]
        lo_b = [(per_tile - 1 - c) * CHUNK for c in range(per_tile)]
        loc_f = [_hgrn_chunk_local(za_ref, row_f + lo, k_f[lo:lo + CHUNK], b_f[lo:lo + CHUNK], False, want,
                                   consts_f) for lo in lo_f]
        loc_b = [_hgrn_chunk_local(za_ref, row_b + lo, k_b[lo:lo + CHUNK], b_b[lo:lo + CHUNK], True, want,
                                   consts_b) for lo in lo_b]
        st_f, st_b = stf_ref[...], stb_ref[...]
        for c in range(per_tile):
            st_f = _hgrn_chunk_carry(o_ref, row_f + lo_f[c], st_f, loc_f[c])
            st_b = _hgrn_chunk_carry(o_ref, row_b + lo_b[c], st_b, loc_b[c])
        stf_ref[...] = st_f
        stb_ref[...] = st_b

    n_ctx_tiles = CTX_LEN // ROW_TILE
    for i in range(n_ctx_tiles):
        tile_step(i * ROW_TILE, (n_ctx_tiles - 1 - i) * ROW_TILE, need_ctx)

    def lat_body(i, carry):
        tile_step(pl.multiple_of((n_ctx_tiles + i) * ROW_TILE, ROW_TILE),
                  pl.multiple_of((n_tiles - 1 - i) * ROW_TILE, ROW_TILE), True)
        return carry

    lax.fori_loop(0, n_tiles - n_ctx_tiles, lat_body, 0)

    gain = gain_ref[...]

    def fin_body(i, carry):
        r0 = pl.multiple_of(i * ROW_TILE, ROW_TILE)
        o = o_ref[0, pl.ds(r0, ROW_TILE), :]
        zo = za_ref[0, pl.ds(r0, ROW_TILE), 2 * GROUP_WIDTH:3 * GROUP_WIDTH]
        o_ref[0, pl.ds(r0, ROW_TILE), :] = _head_rms(o, gain, ones_bd) * (zo * _sigmoid(zo))
        return carry

    lax.fori_loop(0, t // ROW_TILE, fin_body, 0)


def _hgrn(za, lb_logits, gain4, layer, need_ctx):
    b, t, _ = za.shape
    depth = lb_logits.shape[0]
    return pl.pallas_call(
        functools.partial(_hgrn_kernel, layer=layer, need_ctx=need_ctx),
        grid=(b,),
        in_specs=[
            pl.BlockSpec((1, t, A_W), lambda bb: (bb, 0, 0)),
            pl.BlockSpec((depth, 2, GROUP_WIDTH), lambda bb: (0, 0, 0)),
            pl.BlockSpec((1, GROUP_WIDTH), lambda bb: (0, 0)),
        ],
        out_specs=pl.BlockSpec((1, t, GROUP_WIDTH), lambda bb: (bb, 0, 0)),
        out_shape=jax.ShapeDtypeStruct((b, t, GROUP_WIDTH), F32),
        scratch_shapes=[pltpu.VMEM((GROUP_WIDTH, GROUP_WIDTH), F32),
                        pltpu.VMEM((GROUP_WIDTH, GROUP_WIDTH), F32)],
        compiler_params=pltpu.CompilerParams(
            dimension_semantics=("parallel",), vmem_limit_bytes=VMEM_LIMIT_BYTES),
        name="hgrn2_scan",
    )(za, lb_logits, gain4)


def _mlstm_tile_gates(zg_ref, row0, gate_b, consts):
    cmat4, e_i, e_f = consts[0:3]
    hi, mid, lo = _split3(zg_ref[0, pl.ds(row0, ROW_TILE), :] + gate_b)
    ig = _dot(hi, e_i) + _dot(mid, e_i) + _dot(lo, e_i)
    lf = _log_sigmoid(_dot(hi, e_f) + _dot(mid, e_f) + _dot(lo, e_f))
    return ig, _route_l(cmat4, lf)


def _mlstm_chunk_local(zb_ref, row0, ig, bc, rev, want_out, consts):
    ones_bd, bmask, diag, causal = consts[3:7]
    k = zb_ref[0, pl.ds(row0, CHUNK), GROUP_WIDTH:2 * GROUP_WIDTH] * (HEAD_DIM ** -0.5)
    v = zb_ref[0, pl.ds(row0, CHUNK), 2 * GROUP_WIDTH:3 * GROUP_WIDTH]
    b_end = bc[0:1] if rev else bc[CHUNK - 1:CHUNK]
    vb = v.astype(BF16)
    gs = b_end - bc + ig
    g_max = jnp.max(gs, axis=0, keepdims=True)
    kw = k * jnp.exp(gs - g_max)
    loc = dict(b_end=b_end, g_max=g_max, bc=bc, upd=_dot_tn(vb, kw.astype(BF16)) * bmask,
               nupd=jnp.sum(kw, axis=0, keepdims=True), q=None)
    if want_out:
        q = zb_ref[0, pl.ds(row0, CHUNK), 0:GROUP_WIDTH]
        qb = q.astype(BF16)
        rowv = jnp.sum((ig - bc) * diag, axis=0, keepdims=True)
        dlog = jnp.where(causal, bc + rowv, NEG_BIG)
        segs = []
        for hh in range(GROUP_HEADS):
            mh = jnp.max(dlog[:, hh * HEAD_DIM:(hh + 1) * HEAD_DIM], axis=-1, keepdims=True)
            segs.append(jnp.broadcast_to(mh, (CHUNK, HEAD_DIM)))
        m_loc = jnp.concatenate(segs, axis=-1)
        kk = jnp.concatenate([k.astype(BF16)] * GROUP_HEADS, axis=0) * ones_bd
        vv = jnp.concatenate([vb] * GROUP_HEADS, axis=0) * ones_bd
        pb = (jnp.exp(dlog - m_loc) * _dot_nt(qb, kk)).astype(BF16)
        loc.update(q=q, qb=qb, m_loc=m_loc, num_loc=_dot(pb, vv), den_loc=_dot(pb, ones_bd))
    return loc


def _mlstm_chunk_carry(o_ref, row0, state, loc, ones_bd):
    ct, n_prev, m_prev = state
    if loc["q"] is not None:
        qc = _dot_nt(loc["qb"], ct.astype(BF16))
        qn = _dot((loc["q"] * n_prev).astype(BF16), ones_bd)
        a = loc["bc"] + m_prev
        m_t = jnp.maximum(a, loc["m_loc"])
        w_in = jnp.exp(a - m_t)
        w_loc = jnp.exp(loc["m_loc"] - m_t)
        num = w_in * qc + w_loc * loc["num_loc"]
        den = w_in * qn + w_loc * loc["den_loc"]
        o_ref[0, pl.ds(row0, CHUNK), :] += num / jnp.maximum(jnp.abs(den), jnp.exp(-m_t))
    m_new = jnp.maximum(loc["b_end"] + m_prev, loc["g_max"])
    w_old = jnp.exp(loc["b_end"] + m_prev - m_new)
    w_upd = jnp.exp(loc["g_max"] - m_new)
    return ct * w_old + loc["upd"] * w_upd, n_prev * w_old + loc["nupd"] * w_upd, m_new


def _mlstm_kernel(zb_ref, zg_ref, gb_ref, gain_ref, o_ref, ctf_ref, ctb_ref, nf_ref, nb_ref,
                  mf_ref, mb_ref, *, need_ctx):
    t = zb_ref.shape[1]
    n_tiles = t // ROW_TILE
    per_tile = ROW_TILE // CHUNK
    ones_bd = _head_block_ones(GROUP_WIDTH, GROUP_WIDTH, BF16)
    bmask = _head_block_ones(GROUP_WIDTH, GROUP_WIDTH, F32)
    gcol = _iota((128, GROUP_WIDTH), 0)
    ghead = _iota((128, GROUP_WIDTH), 1) >> 6
    route = lambda base: jnp.where(gcol == base + ghead, 1.0, 0.0).astype(BF16)
    tt = _iota((CHUNK, GROUP_WIDTH), 0)
    ss = _iota((CHUNK, GROUP_WIDTH), 1) & (HEAD_DIM - 1)
    diag = jnp.where(tt == ss, 1.0, 0.0).astype(F32)
    r4, c4 = _iota((ROW_TILE, ROW_TILE), 0), _iota((ROW_TILE, ROW_TILE), 1)
    same_chunk = (r4 >> 6) == (c4 >> 6)
    cum4 = lambda rev: jnp.where(same_chunk & ((c4 >= r4) if rev else (c4 <= r4)), 1.0, 0.0).astype(BF16)
    cf = (cum4(False), route(0), route(2 * GROUP_HEADS), ones_bd, bmask, diag, ss <= tt)
    cb = (cum4(True), route(GROUP_HEADS), route(3 * GROUP_HEADS), ones_bd, bmask, diag, ss >= tt)
    gate_b = gb_ref[...]

    for r in (ctf_ref, ctb_ref, nf_ref, nb_ref, mf_ref, mb_ref, o_ref):
        r[...] = jnp.zeros_like(r)

    def tile_step(row_f, row_b, want):
        ig_f, bc_f = _mlstm_tile_gates(zg_ref, row_f, gate_b, cf)
        ig_b, bc_b = _mlstm_tile_gates(zg_ref, row_b, gate_b, cb)
        lo_f = [c * CHUNK for c in range(per_tile)]
        lo_b = [(per_tile - 1 - c) * CHUNK for c in range(per_tile)]
        st_f = (ctf_ref[...], nf_ref[0:1], mf_ref[0:1])
        st_b = (ctb_ref[...], nb_ref[0:1], mb_ref[0:1])
        for c in range(per_tile):
            lo = lo_f[c]
            loc = _mlstm_chunk_local(zb_ref, row_f + lo, ig_f[lo:lo + CHUNK], bc_f[lo:lo + CHUNK], False, want, cf)
            st_f = _mlstm_chunk_carry(o_ref, row_f + lo, st_f, loc, ones_bd)
            lo = lo_b[c]
            loc = _mlstm_chunk_local(zb_ref, row_b + lo, ig_b[lo:lo + CHUNK], bc_b[lo:lo + CHUNK], True, want, cb)
            st_b = _mlstm_chunk_carry(o_ref, row_b + lo, st_b, loc, ones_bd)
        for refs, st in (((ctf_ref, nf_ref, mf_ref), st_f), ((ctb_ref, nb_ref, mb_ref), st_b)):
            refs[0][...] = st[0]
            refs[1][...] = jnp.broadcast_to(st[1], refs[1].shape)
            refs[2][...] = jnp.broadcast_to(st[2], refs[2].shape)

    n_ctx_tiles = CTX_LEN // ROW_TILE
    for i in range(n_ctx_tiles):
        tile_step(i * ROW_TILE, (n_ctx_tiles - 1 - i) * ROW_TILE, need_ctx)

    def lat_body(i, carry):
        tile_step(pl.multiple_of((n_ctx_tiles + i) * ROW_TILE, ROW_TILE),
                  pl.multiple_of((n_tiles - 1 - i) * ROW_TILE, ROW_TILE), True)
        return carry

    lax.fori_loop(0, n_tiles - n_ctx_tiles, lat_body, 0)

    gain = gain_ref[...]

    def fin_body(i, carry):
        r0 = pl.multiple_of(i * ROW_TILE, ROW_TILE)
        o = o_ref[0, pl.ds(r0, ROW_TILE), :]
        zo = zb_ref[0, pl.ds(r0, ROW_TILE), 3 * GROUP_WIDTH:4 * GROUP_WIDTH]
        o_ref[0, pl.ds(r0, ROW_TILE), :] = _head_rms(o, gain, ones_bd) * _sigmoid(zo)
        return carry

    lax.fori_loop(0, t // ROW_TILE, fin_body, 0)


def _mlstm(zb, zg, gate_b128, gain4, need_ctx):
    b, t, _ = zb.shape
    return pl.pallas_call(
        functools.partial(_mlstm_kernel, need_ctx=need_ctx),
        grid=(b,),
        in_specs=[
            pl.BlockSpec((1, t, B_W), lambda bb: (bb, 0, 0)),
            pl.BlockSpec((1, t, G_W), lambda bb: (bb, 0, 0)),
            pl.BlockSpec((1, G_W), lambda bb: (0, 0)),
            pl.BlockSpec((1, GROUP_WIDTH), lambda bb: (0, 0)),
        ],
        out_specs=pl.BlockSpec((1, t, GROUP_WIDTH), lambda bb: (bb, 0, 0)),
        out_shape=jax.ShapeDtypeStruct((b, t, GROUP_WIDTH), F32),
        scratch_shapes=[pltpu.VMEM((GROUP_WIDTH, GROUP_WIDTH), F32),
                        pltpu.VMEM((GROUP_WIDTH, GROUP_WIDTH), F32),
                        pltpu.VMEM((SUB, GROUP_WIDTH), F32), pltpu.VMEM((SUB, GROUP_WIDTH), F32),
                        pltpu.VMEM((SUB, GROUP_WIDTH), F32), pltpu.VMEM((SUB, GROUP_WIDTH), F32)],
        compiler_params=pltpu.CompilerParams(
            dimension_semantics=("parallel",), vmem_limit_bytes=VMEM_LIMIT_BYTES),
        name="mlstm_scan",
    )(zb, zg, gate_b128, gain4)


def _attend_t(qt, pieces):
    return _softmax_pv_t(_scores_t(qt, pieces), pieces)


def _scores_t(qt, pieces):
    s_list = []
    for k, _, bias in pieces:
        s = _dot(k, qt)
        s_list.append(s if bias is None else s + bias)
    return s_list


def _softmax_pv_t(s_list, pieces):
    m = jnp.max(s_list[0], axis=0, keepdims=True)
    for s in s_list[1:]:
        m = jnp.maximum(m, jnp.max(s, axis=0, keepdims=True))
    l = acc = None
    for s, (_, vt, _) in zip(s_list, pieces):
        p = jnp.exp2(s - m)
        ls = jnp.sum(p, axis=0, keepdims=True)
        pv = _dot(vt, p.astype(BF16))
        l = ls if l is None else l + ls
        acc = pv if acc is None else acc + pv
    return acc / l


def _gqa_kernel(q_ref, k_ref, vt_ref, o_ref, *, need_ctx):
    j = pl.program_id(2)
    t = k_ref.shape[2]
    g = q_ref.shape[1]
    nsub = q_ref.shape[3] // ROW_TILE

    def queries(sub):
        return jnp.concatenate([q_ref[0, i, :, sub * ROW_TILE:(sub + 1) * ROW_TILE] for i in range(g)], axis=1)

    def run(has_ctx_tile):
        units = []
        for sub in range(nsub):
            if has_ctx_tile and sub < CTX_LEN // ROW_TILE:
                if need_ctx:
                    units.append((sub, [(k_ref[0, 0, 0:CTX_LEN, :], vt_ref[0, 0, :, 0:CTX_LEN], None)]))
                else:
                    o_ref[0, sub * ROW_TILE:(sub + 1) * ROW_TILE, :] = jnp.zeros((ROW_TILE, g * HEAD_DIM), F32)
            else:
                units.append((sub, [(k_ref[0, 0], vt_ref[0, 0], None)]))
        scores = _scores_t(queries(units[0][0]), units[0][1])
        for n, (sub, pieces) in enumerate(units):
            nxt = None
            if n + 1 < len(units):
                nxt = _scores_t(queries(units[n + 1][0]), units[n + 1][1])
            ot = _softmax_pv_t(scores, pieces)
            o_ref[0, sub * ROW_TILE:(sub + 1) * ROW_TILE, :] = jnp.concatenate(
                [ot[:, i * ROW_TILE:(i + 1) * ROW_TILE] for i in range(g)], axis=0).T
            scores = nxt

    if nsub * ROW_TILE == t:
        run(True)
    else:
        @pl.when(j == 0)
        def _():
            run(True)

        @pl.when(j > 0)
        def _():
            run(False)


def _gqa(qct, kc, vct, need_ctx):
    b, _, t, _ = kc.shape
    g = GROUP_HEADS // KV_HEADS
    qt = GQA_Q_TILES * ROW_TILE
    assert t % qt == 0
    return pl.pallas_call(
        functools.partial(_gqa_kernel, need_ctx=need_ctx),
        grid=(b, KV_HEADS, t // qt),
        in_specs=[
            pl.BlockSpec((1, g, HEAD_DIM, qt), lambda bb, h, j: (bb, h, 0, j)),
            pl.BlockSpec((1, 1, t, HEAD_DIM), lambda bb, h, j: (bb, h, 0, 0)),
            pl.BlockSpec((1, 1, HEAD_DIM, t), lambda bb, h, j: (bb, h, 0, 0)),
        ],
        out_specs=pl.BlockSpec((1, qt, g * HEAD_DIM), lambda bb, h, j: (bb, j, h)),
        out_shape=jax.ShapeDtypeStruct((b, t, GROUP_WIDTH), F32),
        compiler_params=pltpu.CompilerParams(
            dimension_semantics=("parallel", "parallel", "arbitrary"), vmem_limit_bytes=VMEM_LIMIT_BYTES),
        name="gqa_attention",
    )(qct, kc, vct)


def _na_plan(rows):
    kh = min(NA_KH, rows)
    win = min(kh + NA_QROWS, rows)
    assert rows % NA_QROWS == 0 and (win * GRID_W) % 128 == 0
    blocks, variants = [], []
    for i in range(rows // NA_QROWS):
        qrows = np.arange(NA_QROWS * i, NA_QROWS * (i + 1))
        r0 = np.clip(qrows - kh // 2, 0, rows - kh)
        start = int(np.clip(NA_QROWS * i - kh // 2, 0, rows - win))
        assert start % 2 == 0 and start <= r0.min() and r0.max() + kh <= start + win
        key = (start - NA_QROWS * i, tuple(int(v) for v in r0 - qrows))
        if key not in variants:
            variants.append(key)
        blocks.append((start, variants.index(key)))
    return kh, win, tuple(blocks), variants


def _na_kernel(q_ref, k_ref, vt_ref, bias_ref, o_ref, *, need_ctx, blocks):
    nq = NA_QROWS * GRID_W
    nk = bias_ref.shape[2]
    kctx = k_ref[0, 0, 0:CTX_LEN, :]
    vtctx = vt_ref[0, 0, :, 0:CTX_LEN]

    def paired_queries(lo, n):
        q = q_ref[0, 0, :, lo:lo + n]
        z = jnp.zeros((HEAD_DIM, n), q.dtype)
        return jnp.concatenate([jnp.concatenate([q[0:HEAD_DIM], z], axis=1),
                                jnp.concatenate([z, q[HEAD_DIM:2 * HEAD_DIM]], axis=1)], axis=0)

    def unpair(ot, n):
        return jnp.concatenate([ot[0:HEAD_DIM, 0:n], ot[HEAD_DIM:2 * HEAD_DIM, n:2 * n]], axis=0).T

    if need_ctx:
        ot = _attend_t(paired_queries(0, CTX_LEN), [(kctx, vtctx, None)])
        o_ref[0, 0:CTX_LEN, :] = unpair(ot, CTX_LEN)
    else:
        o_ref[0, 0:CTX_LEN, :] = jnp.zeros((CTX_LEN, 2 * HEAD_DIM), F32)

    def block_pieces(i):
        start, var = blocks[i]
        k0 = CTX_LEN + start * GRID_W
        return [(kctx, vtctx, None),
                (k_ref[0, 0, k0:k0 + nk, :], vt_ref[0, 0, :, k0:k0 + nk], bias_ref[0, var])]

    scores = {i: _scores_t(paired_queries(CTX_LEN + i * nq, nq), block_pieces(i))
              for i in range(min(NA_AHEAD, len(blocks)))}
    for i in range(len(blocks)):
        q0 = CTX_LEN + i * nq
        if i + NA_AHEAD < len(blocks):
            scores[i + NA_AHEAD] = _scores_t(paired_queries(q0 + NA_AHEAD * nq, nq), block_pieces(i + NA_AHEAD))
        o_ref[0, q0:q0 + nq, :] = unpair(_softmax_pv_t(scores.pop(i), block_pieces(i)), nq)


def _na(qdt, kd, vdt, bias_t, blocks, need_ctx):
    b, npair, t, pw = kd.shape
    nvar, nk, nq2 = bias_t.shape[1:]
    blk = pl.BlockSpec((1, 1, t, pw), lambda h, bb: (bb, h, 0, 0))
    blk_t = pl.BlockSpec((1, 1, pw, t), lambda h, bb: (bb, h, 0, 0))
    return pl.pallas_call(
        functools.partial(_na_kernel, need_ctx=need_ctx, blocks=blocks),
        grid=(npair, b),
        in_specs=[blk_t, blk, blk_t,
                  pl.BlockSpec((1, nvar, nk, nq2), lambda h, bb: (h, 0, 0, 0))],
        out_specs=pl.BlockSpec((1, t, pw), lambda h, bb: (bb, 0, h)),
        out_shape=jax.ShapeDtypeStruct((b, t, GROUP_WIDTH), F32),
        compiler_params=pltpu.CompilerParams(
            dimension_semantics=("parallel", "arbitrary"), vmem_limit_bytes=VMEM_LIMIT_BYTES),
        name="neighbourhood_attention",
    )(qdt, kd, vdt, bias_t)


def _na_bias_table(rpb, kh, win, variants):
    rows_q = NA_QROWS
    nrow, ncol = 2 * NA_KH - 1, 2 * NA_KW - 1
    col = np.arange(GRID_W)
    cstart = np.clip(col - NA_KW // 2, 0, GRID_W - NA_KW)
    col_in = (col[None, :] >= cstart[:, None]) & (col[None, :] < cstart[:, None] + NA_KW)
    col_idx = np.clip(col[None, :] - col[:, None], 1 - NA_KW, NA_KW - 1) + NA_KW - 1
    col_sel = (col_idx.T[None] == np.arange(ncol)[:, None, None]).astype(np.float32)
    row_sel, valid = [], []
    for ds, offs in variants:
        rel = ds + np.arange(win)[:, None] - np.arange(rows_q)[None, :]
        offs = np.asarray(offs)[None, :]
        valid.append((rel >= offs) & (rel < offs + kh))
        row_sel.append((rel + NA_KH - 1)[..., None] == np.arange(nrow))
    row_sel = np.stack(row_sel).astype(np.float32)
    keep = np.stack(valid)[:, :, None, None, :, None] & col_in.T[None, None, :, None, None, :]
    pairs = rpb.reshape(rpb.shape[0] // 2, 2, nrow, ncol).astype(F32)
    tbl = jnp.einsum('vxar,pgrc,cyq->pvxygaq', row_sel, pairs, col_sel, precision=lax.Precision.HIGHEST)
    tbl = jnp.where(keep[None], tbl * LOG2E, NEG_BIG)
    return tbl.reshape(rpb.shape[0] // 2, len(variants), win * GRID_W, 2 * rows_q * GRID_W)


def _outmlp_kernel(x_ref, a_ref, b_ref, c_ref, d_ref, mlat_ref, mctx_ref, g2_ref, wo_ref, w1_ref,
                   w2_ref, gf_ref, o_ref, *, tile_off, final):
    j = pl.program_id(1) + tile_off
    mod = jnp.where(j == 0, mctx_ref[0], mlat_ref[0])
    cat = jnp.concatenate([a_ref[0], b_ref[0], c_ref[0], d_ref[0]], axis=-1).astype(BF16)
    x1 = x_ref[0] + mod[2:3] * _dot(cat, wo_ref[...])
    ms = jnp.mean(x1 * x1, axis=-1, keepdims=True)
    h2 = ((x1 * lax.rsqrt(ms + EPS) * g2_ref[...]) * (1.0 + mod[4:5]) + mod[3:4]).astype(BF16)
    hid = w1_ref.shape[1]
    step = 1024
    acc = jnp.zeros(x1.shape, F32)
    nchunk = hid // step
    up = _dot(h2, w1_ref[:, 0:step])
    for c in range(nchunk):
        u = jnp.maximum(up, 0.0)
        if c + 1 < nchunk:
            up = _dot(h2, w1_ref[:, (c + 1) * step:(c + 2) * step])
        acc = acc + _dot((u * u).astype(BF16), w2_ref[c * step:(c + 1) * step, :])
    x2 = x1 + mod[5:6] * acc
    if final:
        ms2 = jnp.mean(x2 * x2, axis=-1, keepdims=True)
        x2 = x2 * lax.rsqrt(ms2 + EPS) * gf_ref[...]
    o_ref[0] = x2


def _outmlp(xcat, a, bm, c, dd, mods_l, g2, wo, w1, w2, gf, final):
    b, t, d = xcat.shape
    tile_off = CTX_LEN // ROW_TILE if final else 0
    nt = t // ROW_TILE - tile_off
    ctx_row = mods_l.shape[0] - 1
    full2 = lambda bb, j: (0, 0)
    grp = pl.BlockSpec((1, ROW_TILE, GROUP_WIDTH), lambda bb, j: (bb, j + tile_off, 0))
    return pl.pallas_call(
        functools.partial(_outmlp_kernel, tile_off=tile_off, final=final),
        grid=(b, nt),
        in_specs=[
            pl.BlockSpec((1, ROW_TILE, d), lambda bb, j: (bb, j + tile_off, 0)),
            grp, grp, grp, grp,
            pl.BlockSpec((1, 6, d), lambda bb, j: (bb, 0, 0)),
            pl.BlockSpec((1, 6, d), lambda bb, j: (ctx_row, 0, 0)),
            pl.BlockSpec((1, d), full2),
            pl.BlockSpec(wo.shape, full2),
            pl.BlockSpec(w1.shape, full2),
            pl.BlockSpec(w2.shape, full2),
            pl.BlockSpec((1, d), full2),
        ],
        out_specs=pl.BlockSpec((1, ROW_TILE, d), lambda bb, j: (bb, j, 0)),
        out_shape=jax.ShapeDtypeStruct((b, nt * ROW_TILE, d), F32),
        compiler_params=pltpu.CompilerParams(
            dimension_semantics=("parallel", "arbitrary"), vmem_limit_bytes=VMEM_LIMIT_BYTES),
        name="out_projection_mlp",
    )(xcat, a, bm, c, dd, mods_l, mods_l, g2, wo, w1, w2, gf)


def _pack_in_projection(w):
    gw, kvw = GROUP_WIDTH, KV_HEADS * HEAD_DIM
    g0 = 9 * gw
    c0 = g0 + 4 * GROUP_HEADS
    d0 = c0 + gw + 2 * kvw

    def pair_swapped(cols):
        n = cols.shape[1]
        return cols.reshape(cols.shape[0], n // 2, 2)[:, :, ::-1].reshape(cols.shape[0], n)

    q, k = w[:, c0:c0 + gw], w[:, c0 + gw:c0 + gw + kvw]
    parts = [w[:, 0:g0], w[:, g0:c0], jnp.zeros((w.shape[0], G_W - 4 * GROUP_HEADS), w.dtype),
             q, pair_swapped(q), k, pair_swapped(k), w[:, c0 + gw + kvw:d0], w[:, d0:d0 + 3 * gw]]
    packed = jnp.concatenate(parts, axis=1).astype(BF16)
    assert packed.shape[1] == IN_PACKED
    return packed


def _rope_tables(n_tokens):
    t = np.arange(n_tokens)
    row = (t // GRID_W).astype(np.float32)
    col = (t % GRID_W).astype(np.float32)
    axis_dims = HEAD_DIM // 2
    inv = np.power(np.float32(ROPE_THETA),
                   (-2.0 * np.arange(axis_dims // 2, dtype=np.float32) / axis_dims).astype(np.float32))
    ang = np.concatenate([row[:, None] * inv, col[:, None] * inv], axis=-1).astype(np.float32)
    cos = np.repeat(np.cos(ang).astype(np.float32), 2, axis=-1)
    sign = np.where(np.arange(HEAD_DIM) % 2 == 0, -1.0, 1.0).astype(np.float32)
    sin = np.repeat(np.sin(ang).astype(np.float32), 2, axis=-1) * sign
    cos = np.concatenate([np.ones((CTX_LEN, HEAD_DIM), np.float32), cos], axis=0)
    sin = np.concatenate([np.zeros((CTX_LEN, HEAD_DIM), np.float32), sin], axis=0)
    return jnp.asarray(np.tile(cos, (1, GROUP_HEADS))), jnp.asarray(np.tile(sin, (1, GROUP_HEADS)))


def kernel(x, c, ctx, c_ctx, w_mod, b_mod, norm1_g, norm2_g, w_in, hgrn_lb_logits, hgrn_norm_g,
           mlstm_gate_b, mlstm_norm_g, gqa_qnorm_g, gqa_knorm_g, na_rpb, w_out, w_mlp1, w_mlp2,
           final_norm_g):
    bsz, seq, d = x.shape
    depth = w_mod.shape[0]
    rows = seq // GRID_W

    n_c = bsz + 1
    pad = (-n_c) % 8
    cc = jnp.concatenate([c, c_ctx[None, :], jnp.zeros((pad, d), F32)], axis=0)
    mods = _modulation(cc, w_mod, b_mod)[:, :n_c].reshape(depth, n_c, 6, d)

    cos4, sin4 = _rope_tables(seq)
    swap = np.arange(HEAD_DIM) ^ 1

    xcat = jnp.concatenate([ctx, x], axis=1)
    out = None
    for l in range(depth):
        need_ctx = l < depth - 1
        w_packed = _pack_in_projection(w_in[l])
        gq = jnp.tile(gqa_qnorm_g[l], GROUP_HEADS)[None, :]
        gqs = jnp.tile(gqa_qnorm_g[l][swap], GROUP_HEADS)[None, :]
        gk = jnp.tile(gqa_knorm_g[l], KV_HEADS)[None, :]
        gks = jnp.tile(gqa_knorm_g[l][swap], KV_HEADS)[None, :]
        za, zb, zg, qc, kc, vc, qd, kd, vd = _inproj(
            xcat, mods[l], norm1_g[l][None, :], w_packed, cos4, sin4, gq, gqs, gk, gks)

        a = _hgrn(za, hgrn_lb_logits, jnp.tile(hgrn_norm_g[l], GROUP_HEADS)[None, :], l, need_ctx)
        gate_b = jnp.concatenate([mlstm_gate_b[l], jnp.zeros((G_W - 4 * GROUP_HEADS,), F32)])[None, :]
        bm = _mlstm(zb, zg, gate_b, jnp.tile(mlstm_norm_g[l], GROUP_HEADS)[None, :], need_ctx)
        cg = _gqa(qc, kc, vc, need_ctx)
        kh, win, blocks, variants = _na_plan(rows)
        dg = _na(qd, kd, vd, _na_bias_table(na_rpb[l], kh, win, variants), blocks, need_ctx)

        res = _outmlp(xcat, a, bm, cg, dg, mods[l], norm2_g[l][None, :], w_out[l].astype(BF16),
                      w_mlp1[l].astype(BF16), w_mlp2[l].astype(BF16), final_norm_g[None, :],
                      final=not need_ctx)
        if need_ctx:
            xcat = res
        else:
            out = res
    return out
```

```python
import functools

import numpy as np
import jax
import jax.numpy as jnp
from jax import lax
from jax.experimental import pallas as pl
from jax.experimental.pallas import tpu as pltpu

F32 = jnp.float32
BF16 = jnp.bfloat16

D_MODEL = 1024
GRID_W = 64
CTX_LEN = 256
HEAD_DIM = 64
GROUP_HEADS = 4
GROUP_WIDTH = GROUP_HEADS * HEAD_DIM
KV_HEADS = 2
MLP_HIDDEN = 4 * D_MODEL
CHUNK = 64
SUB = 8
NA_KH = 8
NA_KW = 16
ROPE_THETA = 10000.0
EPS = 1e-6
NEG_BIG = -1e30
FORGET_FLOOR = 1e-20
ROW_TILE = 256
VMEM_LIMIT_BYTES = 56 * 1024 * 1024
LOG2E = 1.4426950408889634
Q_SCALE = HEAD_DIM ** -0.5 * LOG2E
NA_QROWS = 4
NA_AHEAD = 2
GQA_Q_TILES = 3
GQA_AHEAD = 2

A_OFF, A_W = 0, 5 * GROUP_WIDTH
B_OFF, B_W = A_OFF + A_W, 4 * GROUP_WIDTH
G_OFF, G_W = B_OFF + B_W, 128
C_OFF, C_W = G_OFF + G_W, 896
D_OFF, D_W = C_OFF + C_W, 3 * GROUP_WIDTH
IN_PACKED = D_OFF + D_W


def _dot(a, b):
    return jnp.dot(a, b, preferred_element_type=F32)


def _dot_nt(a, b):
    return lax.dot_general(a, b, (((1,), (1,)), ((), ())), preferred_element_type=F32)


def _dot_tn(a, b):
    return lax.dot_general(a, b, (((0,), (0,)), ((), ())), preferred_element_type=F32)


def _split3(x):
    hi = x.astype(BF16)
    r = x - hi.astype(F32)
    mid = r.astype(BF16)
    lo = (r - mid.astype(F32)).astype(BF16)
    return hi, mid, lo


def _route_l(m01, x):
    hi, mid, lo = _split3(x)
    return _dot(m01, hi) + _dot(m01, mid) + _dot(m01, lo)


def _route_r(x, m01):
    hi, mid, lo = _split3(x)
    return _dot(hi, m01) + _dot(mid, m01) + _dot(lo, m01)


def _sigmoid(x):
    return 1.0 / (1.0 + jnp.exp(-x))


def _log_sigmoid(x):
    return -(jnp.maximum(-x, 0.0) + jnp.log1p(jnp.exp(-jnp.abs(x))))


def _iota(shape, dim):
    return lax.broadcasted_iota(jnp.int32, shape, dim)


def _head_block_ones(n, m, dtype):
    return jnp.where((_iota((n, m), 0) >> 6) == (_iota((n, m), 1) >> 6), 1.0, 0.0).astype(dtype)


def _cumsum_matrix(rev):
    r, c = _iota((CHUNK, CHUNK), 0), _iota((CHUNK, CHUNK), 1)
    keep = (c >= r) if rev else (c <= r)
    return jnp.where(keep, 1.0, 0.0).astype(BF16)


def _head_rms(o, gain, ones_bd):
    ssq = _route_r(o * o, ones_bd)
    return o * lax.rsqrt(ssq * (1.0 / HEAD_DIM) + EPS) * gain


def _mod_kernel(c_ref, w_ref, b_ref, o_ref):
    c = c_ref[...]
    s = c * _sigmoid(c)
    o_ref[0] = _dot(s.astype(BF16), w_ref[0].astype(BF16)) + b_ref[0]


def _modulation(cc, w_mod, b_mod):
    depth, d, n = w_mod.shape
    rows = cc.shape[0]
    tn = 1024
    return pl.pallas_call(
        _mod_kernel,
        grid=(depth, n // tn),
        in_specs=[
            pl.BlockSpec((rows, d), lambda l, j: (0, 0)),
            pl.BlockSpec((1, d, tn), lambda l, j: (l, 0, j)),
            pl.BlockSpec((1, 1, tn), lambda l, j: (l, 0, j)),
        ],
        out_specs=pl.BlockSpec((1, rows, tn), lambda l, j: (l, 0, j)),
        out_shape=jax.ShapeDtypeStruct((depth, rows, n), F32),
        compiler_params=pltpu.CompilerParams(
            dimension_semantics=("arbitrary", "arbitrary"), vmem_limit_bytes=VMEM_LIMIT_BYTES),
        name="adaln_modulation",
    )(cc, w_mod, b_mod.reshape(depth, 1, n))


def _inproj_kernel(xc_ref, xl_ref, mlat_ref, mctx_ref, g1_ref, w_ref, cos_ref, sin_ref, gq_ref, gqs_ref,
                   gk_ref, gks_ref, za_ref, zb_ref, zg_ref, qc_ref, kc_ref, vc_ref,
                   qd_ref, kd_ref, vd_ref):
    j = pl.program_id(1)
    mod = jnp.where(j == 0, mctx_ref[0], mlat_ref[0])
    x = jnp.where(j == 0, xc_ref[0], xl_ref[0])
    ms = jnp.mean(x * x, axis=-1, keepdims=True)
    y = x * lax.rsqrt(ms + EPS) * g1_ref[...]
    h = (y * (1.0 + mod[1:2]) + mod[0:1]).astype(BF16)

    zc = _dot(h, w_ref[:, C_OFF:C_OFF + C_W])
    zd = _dot(h, w_ref[:, D_OFF:D_OFF + D_W])
    za_ref[0] = _dot(h, w_ref[:, A_OFF:A_OFF + A_W])

    ones4 = _head_block_ones(GROUP_WIDTH, GROUP_WIDTH, BF16)
    cos, sin = cos_ref[...], sin_ref[...]
    q, qs = zc[:, 0:256], zc[:, 256:512]
    k, ks = zc[:, 512:640], zc[:, 640:768]
    ssq_q = _route_r(q * q, ones4)
    ssq_k = _route_r(k * k, ones4[0:128, 0:128])
    zb_ref[0] = _dot(h, w_ref[:, B_OFF:B_OFF + B_W])
    zg_ref[0] = _dot(h, w_ref[:, G_OFF:G_OFF + G_W])
    rq = lax.rsqrt(ssq_q * (1.0 / HEAD_DIM) + EPS)
    qrt = ((rq * (q * gq_ref[...] * cos + qs * gqs_ref[...] * sin)) * Q_SCALE).T
    rk = lax.rsqrt(ssq_k * (1.0 / HEAD_DIM) + EPS)
    kr = (rk * (k * gk_ref[...] * cos[:, 0:128] + ks * gks_ref[...] * sin[:, 0:128])).astype(BF16)
    vvt = zc[:, 768:896].T
    for hh in range(GROUP_HEADS):
        qc_ref[0, hh] = qrt[hh * HEAD_DIM:(hh + 1) * HEAD_DIM].astype(BF16)
    for hh in range(KV_HEADS):
        kc_ref[0, hh] = kr[:, hh * HEAD_DIM:(hh + 1) * HEAD_DIM]
        vc_ref[0, hh] = vvt[hh * HEAD_DIM:(hh + 1) * HEAD_DIM].astype(BF16)

    qdt = (zd[:, 0:256] * Q_SCALE).T
    kdb = zd[:, 256:512].astype(BF16)
    vdt = zd[:, 512:768].T
    for pp in range(GROUP_HEADS // 2):
        lo, hi = pp * 2 * HEAD_DIM, (pp + 1) * 2 * HEAD_DIM
        qd_ref[0, pp] = qdt[lo:hi].astype(BF16)
        kd_ref[0, pp] = kdb[:, lo:hi]
        vd_ref[0, pp] = vdt[lo:hi].astype(BF16)


def _token_sources(x_ctx, x_lat, tile_off=0):
    assert CTX_LEN == ROW_TILE
    d = x_lat.shape[-1]
    lat_shift = 0 if x_lat is x_ctx else 1
    return [pl.BlockSpec((1, ROW_TILE, d), lambda bb, j: (bb, 0, 0)),
            pl.BlockSpec((1, ROW_TILE, d), lambda bb, j: (bb, jnp.maximum(j + tile_off, 1) - lat_shift, 0))]


def _inproj(x_ctx, x_lat, mods_l, g1, w_packed, cos4, sin4, gq, gqs, gk, gks):
    b, _, d = x_lat.shape
    t = x_lat.shape[1] if x_lat is x_ctx else x_ctx.shape[1] + x_lat.shape[1]
    nt = t // ROW_TILE
    ctx_row = mods_l.shape[0] - 1
    full2 = lambda bb, j: (0, 0)
    headed = lambda nh, w=HEAD_DIM: pl.BlockSpec((1, nh, ROW_TILE, w), lambda bb, j: (bb, 0, j, 0))
    headed_t = lambda nh, w=HEAD_DIM: pl.BlockSpec((1, nh, w, ROW_TILE), lambda bb, j: (bb, 0, 0, j))
    flat = lambda w: pl.BlockSpec((1, ROW_TILE, w), lambda bb, j: (bb, j, 0))
    hshape = lambda nh, w=HEAD_DIM: jax.ShapeDtypeStruct((b, nh, t, w), BF16)
    hshape_t = lambda nh, w=HEAD_DIM: jax.ShapeDtypeStruct((b, nh, w, t), BF16)
    fshape = lambda w: jax.ShapeDtypeStruct((b, t, w), F32)
    pw = 2 * HEAD_DIM
    return pl.pallas_call(
        _inproj_kernel,
        grid=(b, nt),
        in_specs=_token_sources(x_ctx, x_lat) + [
            pl.BlockSpec((1, 6, d), lambda bb, j: (bb, 0, 0)),
            pl.BlockSpec((1, 6, d), lambda bb, j: (ctx_row, 0, 0)),
            pl.BlockSpec((1, d), full2),
            pl.BlockSpec((d, IN_PACKED), full2),
            pl.BlockSpec((ROW_TILE, GROUP_WIDTH), lambda bb, j: (j, 0)),
            pl.BlockSpec((ROW_TILE, GROUP_WIDTH), lambda bb, j: (j, 0)),
            pl.BlockSpec((1, GROUP_WIDTH), full2),
            pl.BlockSpec((1, GROUP_WIDTH), full2),
            pl.BlockSpec((1, 128), full2),
            pl.BlockSpec((1, 128), full2),
        ],
        out_specs=[flat(A_W), flat(B_W), flat(G_W), headed_t(4), headed(2), headed_t(2),
                   headed_t(2, pw), headed(2, pw), headed_t(2, pw)],
        out_shape=[fshape(A_W), fshape(B_W), fshape(G_W), hshape_t(4), hshape(2), hshape_t(2),
                   hshape_t(2, pw), hshape(2, pw), hshape_t(2, pw)],
        compiler_params=pltpu.CompilerParams(
            dimension_semantics=("parallel", "arbitrary"), vmem_limit_bytes=VMEM_LIMIT_BYTES),
        name="in_projection",
    )(x_ctx, x_lat, mods_l, mods_l, g1, w_packed, cos4, sin4, gq, gqs, gk, gks)


def _hgrn_keys_decay(za_ref, row0, lb, rev):
    fcol = 4 * GROUP_WIDTH if rev else 3 * GROUP_WIDTH
    zf = za_ref[0, pl.ds(row0, ROW_TILE), fcol:fcol + GROUP_WIDTH]
    sg = _sigmoid(zf)
    g = jnp.log(jnp.maximum(lb + (1.0 - lb) * sg, FORGET_FLOOR))
    return (1.0 - lb) * (1.0 - sg), g


def _hgrn_tile_gates(za_ref, row0, lb, rev, cmat4):
    k, g = _hgrn_keys_decay(za_ref, row0, lb, rev)
    return k, _route_l(cmat4, g)


HGRN_SCALES = (1, 2, 4, 8, 16, 32)


def _hgrn_scale_consts(rev):
    t = _iota((CHUNK, GROUP_WIDTH), 0)
    s = _iota((CHUNK, GROUP_WIDTH), 1) & (HEAD_DIM - 1)
    out = []
    for c in HGRN_SCALES:
        sh = c.bit_length()
        t_hi = (t & (2 * c - 1)) >= c
        s_hi = (s & (2 * c - 1)) >= c
        isq = jnp.logical_not(t_hi) if rev else t_hi
        iskey = s_hi if rev else jnp.logical_not(s_hi)
        pair = jnp.where(((t >> sh) == (s >> sh)) & isq & iskey, 1.0, 0.0).astype(F32)
        out.append((isq, pair))
    return out, jnp.where(t == s, 1.0, 0.0).astype(F32)


def _hgrn_boundary_rows(b, c, rev):
    off = c if rev else c - 1
    if 2 * c >= SUB:
        pieces = [jnp.broadcast_to(b[lo + off:lo + off + 1], (2 * c, b.shape[1]))
                  for lo in range(0, CHUNK, 2 * c)]
        return pieces[0] if len(pieces) == 1 else jnp.concatenate(pieces, axis=0)
    pos = _iota(b.shape, 0) & (2 * c - 1)
    out = b
    for j in range(2 * c):
        if j != off:
            out = jnp.where(pos == j, pltpu.roll(b, (j - off) % CHUNK, 0), out)
    return out


def _hgrn_chunk_local(za_ref, row0, k, b, rev, want_out, consts):
    ones_bd, bmask, scales, diag = consts
    v = za_ref[0, pl.ds(row0, CHUNK), GROUP_WIDTH:2 * GROUP_WIDTH]
    b_end = b[0:1] if rev else b[CHUNK - 1:CHUNK]
    vb = v.astype(BF16)
    upd = _dot_tn(vb, (k * jnp.exp(b_end - b)).astype(BF16)) * bmask
    if not want_out:
        return jnp.exp(b_end), upd, None, None
    zq = za_ref[0, pl.ds(row0, CHUNK), 0:GROUP_WIDTH]
    q = zq * _sigmoid(zq) * (HEAD_DIM ** -0.5)
    stack = lambda x: jnp.concatenate([x] * GROUP_HEADS, axis=0) * ones_bd
    a = _dot_nt(q.astype(BF16), stack(k.astype(BF16))) * diag
    for c, (isq, pair) in zip(HGRN_SCALES, scales):
        d = b - _hgrn_boundary_rows(b, c, rev)
        x = (jnp.where(isq, q, k) * jnp.exp(jnp.where(isq, d, -d))).astype(BF16)
        a = a + _dot_nt(x, stack(x)) * pair
    return jnp.exp(b_end), upd, (q * jnp.exp(b)).astype(BF16), _dot(a.astype(BF16), stack(vb))


def _hgrn_chunk_carry(o_ref, row0, st, local):
    decay, upd, qe, o_intra = local
    if qe is not None:
        o_ref[0, pl.ds(row0, CHUNK), :] += o_intra + _dot_nt(qe, st.astype(BF16))
    return st * decay + upd


def _hgrn_kernel(za_ref, lg_ref, gain_ref, o_ref, stf_ref, stb_ref, gates_ref, *, layer, need_ctx):
    t = za_ref.shape[1]
    n_tiles = t // ROW_TILE
    per_tile = ROW_TILE // CHUNK
    lg = lg_ref[...]
    depth = lg.shape[0]
    mx = lg[0]
    for i in range(1, depth):
        mx = jnp.maximum(mx, lg[i])
    es = [jnp.exp(lg[i] - mx) for i in range(depth)]
    tot = es[0]
    for i in range(1, depth):
        tot = tot + es[i]
    sm = [e / tot for e in es]
    cs = sm[0]
    for i in range(1, layer + 1):
        cs = cs + sm[i]
    lbs = cs - sm[0]
    lb_f, lb_b = lbs[0:1], lbs[1:2]

    ones_bd = _head_block_ones(GROUP_WIDTH, GROUP_WIDTH, BF16)
    bmask = _head_block_ones(GROUP_WIDTH, GROUP_WIDTH, F32)
    consts_f = (ones_bd, bmask) + _hgrn_scale_consts(False)
    consts_b = (ones_bd, bmask) + _hgrn_scale_consts(True)
    r4, c4 = _iota((ROW_TILE, ROW_TILE), 0), _iota((ROW_TILE, ROW_TILE), 1)
    same_chunk = (r4 >> 6) == (c4 >> 6)
    cum_f = jnp.where(same_chunk & (c4 <= r4), 1.0, 0.0).astype(BF16)
    cum_b = jnp.where(same_chunk & (c4 >= r4), 1.0, 0.0).astype(BF16)

    stf_ref[...] = jnp.zeros_like(stf_ref)
    stb_ref[...] = jnp.zeros_like(stb_ref)
    o_ref[...] = jnp.zeros_like(o_ref)

    def tile_step(row_f, row_b, want, gates, next_rows):
        k_f, b_f, k_b, b_b = gates
        kn_f, gn_f = _hgrn_keys_decay(za_ref, next_rows[0], lb_f, False)
        kn_b, gn_b = _hgrn_keys_decay(za_ref, next_rows[1], lb_b, True)
        lo_f = [c * CHUNK for c in range(per_tile)]
        lo_b = [(per_tile - 1 - c) * CHUNK for c in range(per_tile)]
        loc_f = [_hgrn_chunk_local(za_ref, row_f + lo, k_f[lo:lo + CHUNK], b_f[lo:lo + CHUNK], False, want,
                                   consts_f) for lo in lo_f]
        bn_f = _route_l(cum_f, gn_f)
        loc_b = [_hgrn_chunk_local(za_ref, row_b + lo, k_b[lo:lo + CHUNK], b_b[lo:lo + CHUNK], True, want,
                                   consts_b) for lo in lo_b]
        bn_b = _route_l(cum_b, gn_b)
        st_f, st_b = stf_ref[...], stb_ref[...]
        for c in range(per_tile):
            st_f = _hgrn_chunk_carry(o_ref, row_f + lo_f[c], st_f, loc_f[c])
            st_b = _hgrn_chunk_carry(o_ref, row_b + lo_b[c], st_b, loc_b[c])
        stf_ref[...] = st_f
        stb_ref[...] = st_b
        return kn_f, bn_f, kn_b, bn_b

    n_ctx_tiles = CTX_LEN // ROW_TILE
    lat_rows = lambda i: (pl.multiple_of(jnp.minimum(n_ctx_tiles + i, n_tiles - 1) * ROW_TILE, ROW_TILE),
                          pl.multiple_of(jnp.maximum(n_tiles - 1 - i, 0) * ROW_TILE, ROW_TILE))

    def park(gates):
        for n, g in enumerate(gates):
            gates_ref[n] = g

    gates = (_hgrn_tile_gates(za_ref, 0, lb_f, False, cum_f)
             + _hgrn_tile_gates(za_ref, (n_ctx_tiles - 1) * ROW_TILE, lb_b, True, cum_b))
    for i in range(n_ctx_tiles):
        if i + 1 < n_ctx_tiles:
            next_rows = ((i + 1) * ROW_TILE, (n_ctx_tiles - 2 - i) * ROW_TILE)
        else:
            next_rows = lat_rows(0)
        gates = tile_step(i * ROW_TILE, (n_ctx_tiles - 1 - i) * ROW_TILE, need_ctx, gates, next_rows)
    park(gates)

    def lat_body(i, carry):
        row_f, row_b = lat_rows(i)
        park(tile_step(row_f, row_b, True, tuple(gates_ref[n] for n in range(4)), lat_rows(i + 1)))
        return carry

    lax.fori_loop(0, n_tiles - n_ctx_tiles, lat_body, 0)

    gain = gain_ref[...]

    def fin_body(i, carry):
        r0 = pl.multiple_of(i * ROW_TILE, ROW_TILE)
        o = o_ref[0, pl.ds(r0, ROW_TILE), :]
        zo = za_ref[0, pl.ds(r0, ROW_TILE), 2 * GROUP_WIDTH:3 * GROUP_WIDTH]
        o_ref[0, pl.ds(r0, ROW_TILE), :] = _head_rms(o, gain, ones_bd) * (zo * _sigmoid(zo))
        return carry

    lax.fori_loop(0, t // ROW_TILE, fin_body, 0)


def _hgrn(za, lb_logits, gain4, layer, need_ctx):
    b, t, _ = za.shape
    depth = lb_logits.shape[0]
    return pl.pallas_call(
        functools.partial(_hgrn_kernel, layer=layer, need_ctx=need_ctx),
        grid=(b,),
        in_specs=[
            pl.BlockSpec((1, t, A_W), lambda bb: (bb, 0, 0)),
            pl.BlockSpec((depth, 2, GROUP_WIDTH), lambda bb: (0, 0, 0)),
            pl.BlockSpec((1, GROUP_WIDTH), lambda bb: (0, 0)),
        ],
        out_specs=pl.BlockSpec((1, t, GROUP_WIDTH), lambda bb: (bb, 0, 0)),
        out_shape=jax.ShapeDtypeStruct((b, t, GROUP_WIDTH), F32),
        scratch_shapes=[pltpu.VMEM((GROUP_WIDTH, GROUP_WIDTH), F32),
                        pltpu.VMEM((GROUP_WIDTH, GROUP_WIDTH), F32),
                        pltpu.VMEM((4, ROW_TILE, GROUP_WIDTH), F32)],
        compiler_params=pltpu.CompilerParams(
            dimension_semantics=("parallel",), vmem_limit_bytes=VMEM_LIMIT_BYTES),
        name="hgrn2_scan",
    )(za, lb_logits, gain4)


def _mlstm_gate_routes(zg_ref, row0, gate_b, consts):
    e_i, e_f = consts[1:3]
    hi, mid, lo = _split3(zg_ref[0, pl.ds(row0, ROW_TILE), :] + gate_b)
    ig = _dot(hi, e_i) + _dot(mid, e_i) + _dot(lo, e_i)
    return ig, _log_sigmoid(_dot(hi, e_f) + _dot(mid, e_f) + _dot(lo, e_f))


def _mlstm_tile_gates(zg_ref, row0, gate_b, consts):
    ig, lf = _mlstm_gate_routes(zg_ref, row0, gate_b, consts)
    return ig, _route_l(consts[0], lf)


def _mlstm_chunk_local(zb_ref, row0, ig, bc, rev, want_out, consts):
    ones_bd, bmask, diag, causal = consts[3:7]
    k = zb_ref[0, pl.ds(row0, CHUNK), GROUP_WIDTH:2 * GROUP_WIDTH] * (HEAD_DIM ** -0.5)
    v = zb_ref[0, pl.ds(row0, CHUNK), 2 * GROUP_WIDTH:3 * GROUP_WIDTH]
    b_end = bc[0:1] if rev else bc[CHUNK - 1:CHUNK]
    vb = v.astype(BF16)
    gs = b_end - bc + ig
    g_max = jnp.max(gs, axis=0, keepdims=True)
    kw = k * jnp.exp(gs - g_max)
    loc = dict(b_end=b_end, g_max=g_max, bc=bc, upd=_dot_tn(vb, kw.astype(BF16)) * bmask,
               nupd=jnp.sum(kw, axis=0, keepdims=True), q=None)
    if want_out:
        q = zb_ref[0, pl.ds(row0, CHUNK), 0:GROUP_WIDTH]
        qb = q.astype(BF16)
        rowv = jnp.sum((ig - bc) * diag, axis=0, keepdims=True)
        dlog = jnp.where(causal, bc + rowv, NEG_BIG)
        segs = []
        for hh in range(GROUP_HEADS):
            mh = jnp.max(dlog[:, hh * HEAD_DIM:(hh + 1) * HEAD_DIM], axis=-1, keepdims=True)
            segs.append(jnp.broadcast_to(mh, (CHUNK, HEAD_DIM)))
        m_loc = jnp.concatenate(segs, axis=-1)
        kk = jnp.concatenate([k.astype(BF16)] * GROUP_HEADS, axis=0) * ones_bd
        vv = jnp.concatenate([vb] * GROUP_HEADS, axis=0) * ones_bd
        pb = (jnp.exp(dlog - m_loc) * _dot_nt(qb, kk)).astype(BF16)
        loc.update(q=q, qb=qb, m_loc=m_loc, num_loc=_dot(pb, vv), den_loc=_dot(pb, ones_bd))
    return loc


def _mlstm_chunk_carry(o_ref, row0, state, loc, ones_bd):
    ct, n_prev, m_prev = state
    if loc["q"] is not None:
        qc = _dot_nt(loc["qb"], ct.astype(BF16))
        qn = _dot((loc["q"] * n_prev).astype(BF16), ones_bd)
        a = loc["bc"] + m_prev
        m_t = jnp.maximum(a, loc["m_loc"])
        w_in = jnp.exp(a - m_t)
        w_loc = jnp.exp(loc["m_loc"] - m_t)
        num = w_in * qc + w_loc * loc["num_loc"]
        den = w_in * qn + w_loc * loc["den_loc"]
        o_ref[0, pl.ds(row0, CHUNK), :] += num / jnp.maximum(jnp.abs(den), jnp.exp(-m_t))
    m_new = jnp.maximum(loc["b_end"] + m_prev, loc["g_max"])
    w_old = jnp.exp(loc["b_end"] + m_prev - m_new)
    w_upd = jnp.exp(loc["g_max"] - m_new)
    return ct * w_old + loc["upd"] * w_upd, n_prev * w_old + loc["nupd"] * w_upd, m_new


def _mlstm_kernel(zb_ref, zg_ref, gb_ref, gain_ref, o_ref, ctf_ref, ctb_ref, nf_ref, nb_ref,
                  mf_ref, mb_ref, gates_ref, *, need_ctx):
    t = zb_ref.shape[1]
    n_tiles = t // ROW_TILE
    per_tile = ROW_TILE // CHUNK
    ones_bd = _head_block_ones(GROUP_WIDTH, GROUP_WIDTH, BF16)
    bmask = _head_block_ones(GROUP_WIDTH, GROUP_WIDTH, F32)
    gcol = _iota((128, GROUP_WIDTH), 0)
    ghead = _iota((128, GROUP_WIDTH), 1) >> 6
    route = lambda base: jnp.where(gcol == base + ghead, 1.0, 0.0).astype(BF16)
    tt = _iota((CHUNK, GROUP_WIDTH), 0)
    ss = _iota((CHUNK, GROUP_WIDTH), 1) & (HEAD_DIM - 1)
    diag = jnp.where(tt == ss, 1.0, 0.0).astype(F32)
    r4, c4 = _iota((ROW_TILE, ROW_TILE), 0), _iota((ROW_TILE, ROW_TILE), 1)
    same_chunk = (r4 >> 6) == (c4 >> 6)
    cum4 = lambda rev: jnp.where(same_chunk & ((c4 >= r4) if rev else (c4 <= r4)), 1.0, 0.0).astype(BF16)
    cf = (cum4(False), route(0), route(2 * GROUP_HEADS), ones_bd, bmask, diag, ss <= tt)
    cb = (cum4(True), route(GROUP_HEADS), route(3 * GROUP_HEADS), ones_bd, bmask, diag, ss >= tt)
    gate_b = gb_ref[...]

    for r in (ctf_ref, ctb_ref, nf_ref, nb_ref, mf_ref, mb_ref, o_ref):
        r[...] = jnp.zeros_like(r)

    def tile_gates(row_f, row_b):
        return (_mlstm_tile_gates(zg_ref, row_f, gate_b, cf) + _mlstm_tile_gates(zg_ref, row_b, gate_b, cb))

    def tile_step(row_f, row_b, want, gates, next_rows=None):
        ig_f, bc_f, ig_b, bc_b = gates
        lo_f = [c * CHUNK for c in range(per_tile)]
        lo_b = [(per_tile - 1 - c) * CHUNK for c in range(per_tile)]
        st_f = (ctf_ref[...], nf_ref[0:1], mf_ref[0:1])
        st_b = (ctb_ref[...], nb_ref[0:1], mb_ref[0:1])
        nxt = None
        for c in range(per_tile):
            if next_rows is not None and c == 1:
                nxt = [_mlstm_gate_routes(zg_ref, r, gate_b, cc) for r, cc in zip(next_rows, (cf, cb))]
            if next_rows is not None and c == 3:
                nxt = (nxt[0][0], _route_l(cf[0], nxt[0][1]), nxt[1][0], _route_l(cb[0], nxt[1][1]))
            lo = lo_f[c]
            loc = _mlstm_chunk_local(zb_ref, row_f + lo, ig_f[lo:lo + CHUNK], bc_f[lo:lo + CHUNK], False, want, cf)
            st_f = _mlstm_chunk_carry(o_ref, row_f + lo, st_f, loc, ones_bd)
            lo = lo_b[c]
            loc = _mlstm_chunk_local(zb_ref, row_b + lo, ig_b[lo:lo + CHUNK], bc_b[lo:lo + CHUNK], True, want, cb)
            st_b = _mlstm_chunk_carry(o_ref, row_b + lo, st_b, loc, ones_bd)
        for refs, st in (((ctf_ref, nf_ref, mf_ref), st_f), ((ctb_ref, nb_ref, mb_ref), st_b)):
            refs[0][...] = st[0]
            refs[1][...] = jnp.broadcast_to(st[1], refs[1].shape)
            refs[2][...] = jnp.broadcast_to(st[2], refs[2].shape)
        return nxt

    n_ctx_tiles = CTX_LEN // ROW_TILE
    n_lat = n_tiles - n_ctx_tiles
    lat_rows = lambda i: (pl.multiple_of(jnp.minimum(n_ctx_tiles + i, n_tiles - 1) * ROW_TILE, ROW_TILE),
                          pl.multiple_of(jnp.maximum(n_tiles - 1 - i, 0) * ROW_TILE, ROW_TILE))

    def park(gates):
        for n, g in enumerate(gates):
            gates_ref[n] = g

    gates = tile_gates(0, (n_ctx_tiles - 1) * ROW_TILE)
    for i in range(n_ctx_tiles):
        if i + 1 < n_ctx_tiles:
            next_rows = ((i + 1) * ROW_TILE, (n_ctx_tiles - 2 - i) * ROW_TILE)
        else:
            next_rows = lat_rows(0)
        gates = tile_step(i * ROW_TILE, (n_ctx_tiles - 1 - i) * ROW_TILE, need_ctx, gates, next_rows)
    park(gates)

    def lat_body(i, carry):
        row_f, row_b = lat_rows(i)
        park(tile_step(row_f, row_b, True, tuple(gates_ref[n] for n in range(4)), lat_rows(i + 1)))
        return carry

    lax.fori_loop(0, n_lat, lat_body, 0)

    gain = gain_ref[...]

    def fin_body(i, carry):
        r0 = pl.multiple_of(i * ROW_TILE, ROW_TILE)
        o = o_ref[0, pl.ds(r0, ROW_TILE), :]
        zo = zb_ref[0, pl.ds(r0, ROW_TILE), 3 * GROUP_WIDTH:4 * GROUP_WIDTH]
        o_ref[0, pl.ds(r0, ROW_TILE), :] = _head_rms(o, gain, ones_bd) * _sigmoid(zo)
        return carry

    lax.fori_loop(0, t // ROW_TILE, fin_body, 0)


def _mlstm(zb, zg, gate_b128, gain4, need_ctx):
    b, t, _ = zb.shape
    return pl.pallas_call(
        functools.partial(_mlstm_kernel, need_ctx=need_ctx),
        grid=(b,),
        in_specs=[
            pl.BlockSpec((1, t, B_W), lambda bb: (bb, 0, 0)),
            pl.BlockSpec((1, t, G_W), lambda bb: (bb, 0, 0)),
            pl.BlockSpec((1, G_W), lambda bb: (0, 0)),
            pl.BlockSpec((1, GROUP_WIDTH), lambda bb: (0, 0)),
        ],
        out_specs=pl.BlockSpec((1, t, GROUP_WIDTH), lambda bb: (bb, 0, 0)),
        out_shape=jax.ShapeDtypeStruct((b, t, GROUP_WIDTH), F32),
        scratch_shapes=[pltpu.VMEM((GROUP_WIDTH, GROUP_WIDTH), F32),
                        pltpu.VMEM((GROUP_WIDTH, GROUP_WIDTH), F32),
                        pltpu.VMEM((SUB, GROUP_WIDTH), F32), pltpu.VMEM((SUB, GROUP_WIDTH), F32),
                        pltpu.VMEM((SUB, GROUP_WIDTH), F32), pltpu.VMEM((SUB, GROUP_WIDTH), F32),
                        pltpu.VMEM((4, ROW_TILE, GROUP_WIDTH), F32)],
        compiler_params=pltpu.CompilerParams(
            dimension_semantics=("parallel",), vmem_limit_bytes=VMEM_LIMIT_BYTES),
        name="mlstm_scan",
    )(zb, zg, gate_b128, gain4)


def _attend_t(qt, pieces):
    return _softmax_pv_t(_scores_t(qt, pieces), pieces, True)


def _scores_t(qt, pieces):
    s_list = []
    for k, _, bias in pieces:
        s = _dot(k, qt)
        s_list.append(s if bias is None else s + bias)
    return s_list


def _softmax_pv_t(s_list, pieces, mxu_denominator):
    m = jnp.max(s_list[0], axis=0, keepdims=True)
    for s in s_list[1:]:
        m = jnp.maximum(m, jnp.max(s, axis=0, keepdims=True))
    if mxu_denominator:
        pad = 2 * SUB
        acc = None
        for s, (_, vt, _) in zip(s_list, pieces):
            vt1 = jnp.concatenate([vt, jnp.ones((pad, vt.shape[1]), vt.dtype)], axis=0)
            pv = _dot(vt1, jnp.exp2(s - m).astype(BF16))
            acc = pv if acc is None else acc + pv
        hd = acc.shape[0] - pad
        return acc[0:hd] / acc[hd:hd + 1]
    l = acc = None
    for s, (_, vt, _) in zip(s_list, pieces):
        p = jnp.exp2(s - m)
        ls = jnp.sum(p, axis=0, keepdims=True)
        pv = _dot(vt, p.astype(BF16))
        l = ls if l is None else l + ls
        acc = pv if acc is None else acc + pv
    return acc / l


def _gqa_kernel(q_ref, k_ref, vt_ref, o_ref, *, need_ctx):
    j = pl.program_id(2)
    t = k_ref.shape[2]
    g = q_ref.shape[1]
    nsub = q_ref.shape[3] // ROW_TILE

    def queries(sub):
        return jnp.concatenate([q_ref[0, i, :, sub * ROW_TILE:(sub + 1) * ROW_TILE] for i in range(g)], axis=1)

    def run(has_ctx_tile):
        units = []
        for sub in range(nsub):
            if has_ctx_tile and sub < CTX_LEN // ROW_TILE:
                if need_ctx:
                    units.append((sub, [(k_ref[0, 0, 0:CTX_LEN, :], vt_ref[0, 0, :, 0:CTX_LEN], None)]))
                else:
                    o_ref[0, sub * ROW_TILE:(sub + 1) * ROW_TILE, :] = jnp.zeros((ROW_TILE, g * HEAD_DIM), F32)
            else:
                units.append((sub, [(k_ref[0, 0], vt_ref[0, 0], None)]))
        scores = {n: _scores_t(queries(units[n][0]), units[n][1]) for n in range(min(GQA_AHEAD, len(units)))}
        for n, (sub, pieces) in enumerate(units):
            if n + GQA_AHEAD < len(units):
                scores[n + GQA_AHEAD] = _scores_t(queries(units[n + GQA_AHEAD][0]), units[n + GQA_AHEAD][1])
            ot = _softmax_pv_t(scores.pop(n), pieces, False)
            o_ref[0, sub * ROW_TILE:(sub + 1) * ROW_TILE, :] = jnp.concatenate(
                [ot[:, i * ROW_TILE:(i + 1) * ROW_TILE] for i in range(g)], axis=0).T

    if nsub * ROW_TILE == t:
        run(True)
    else:
        @pl.when(j == 0)
        def _():
            run(True)

        @pl.when(j > 0)
        def _():
            run(False)


def _gqa(qct, kc, vct, need_ctx):
    b, _, t, _ = kc.shape
    g = GROUP_HEADS // KV_HEADS
    qt = GQA_Q_TILES * ROW_TILE
    assert t % qt == 0
    return pl.pallas_call(
        functools.partial(_gqa_kernel, need_ctx=need_ctx),
        grid=(b, KV_HEADS, t // qt),
        in_specs=[
            pl.BlockSpec((1, g, HEAD_DIM, qt), lambda bb, h, j: (bb, h, 0, j)),
            pl.BlockSpec((1, 1, t, HEAD_DIM), lambda bb, h, j: (bb, h, 0, 0)),
            pl.BlockSpec((1, 1, HEAD_DIM, t), lambda bb, h, j: (bb, h, 0, 0)),
        ],
        out_specs=pl.BlockSpec((1, qt, g * HEAD_DIM), lambda bb, h, j: (bb, j, h)),
        out_shape=jax.ShapeDtypeStruct((b, t, GROUP_WIDTH), F32),
        compiler_params=pltpu.CompilerParams(
            dimension_semantics=("parallel", "parallel", "arbitrary"), vmem_limit_bytes=VMEM_LIMIT_BYTES),
        name="gqa_attention",
    )(qct, kc, vct)


def _na_plan(rows):
    kh = min(NA_KH, rows)
    win = min(kh + NA_QROWS, rows)
    assert rows % NA_QROWS == 0 and (win * GRID_W) % 128 == 0
    blocks, variants = [], []
    for i in range(rows // NA_QROWS):
        qrows = np.arange(NA_QROWS * i, NA_QROWS * (i + 1))
        r0 = np.clip(qrows - kh // 2, 0, rows - kh)
        start = int(np.clip(NA_QROWS * i - kh // 2, 0, rows - win))
        assert start % 2 == 0 and start <= r0.min() and r0.max() + kh <= start + win
        key = (start - NA_QROWS * i, tuple(int(v) for v in r0 - qrows))
        if key not in variants:
            variants.append(key)
        blocks.append((start, variants.index(key)))
    return kh, win, tuple(blocks), variants


def _na_kernel(q_ref, k_ref, vt_ref, bias_ref, o_ref, *, need_ctx, blocks):
    nq = NA_QROWS * GRID_W
    nk = bias_ref.shape[2]
    kctx = k_ref[0, 0, 0:CTX_LEN, :]
    vtctx = vt_ref[0, 0, :, 0:CTX_LEN]

    def paired_queries(lo, n):
        q = q_ref[0, 0, :, lo:lo + n]
        z = jnp.zeros((HEAD_DIM, n), q.dtype)
        return jnp.concatenate([jnp.concatenate([q[0:HEAD_DIM], z], axis=1),
                                jnp.concatenate([z, q[HEAD_DIM:2 * HEAD_DIM]], axis=1)], axis=0)

    def unpair(ot, n):
        return jnp.concatenate([ot[0:HEAD_DIM, 0:n], ot[HEAD_DIM:2 * HEAD_DIM, n:2 * n]], axis=0).T

    if need_ctx:
        ot = _attend_t(paired_queries(0, CTX_LEN), [(kctx, vtctx, None)])
        o_ref[0, 0:CTX_LEN, :] = unpair(ot, CTX_LEN)
    else:
        o_ref[0, 0:CTX_LEN, :] = jnp.zeros((CTX_LEN, 2 * HEAD_DIM), F32)

    def block_pieces(i):
        start, var = blocks[i]
        k0 = CTX_LEN + start * GRID_W
        return [(kctx, vtctx, None),
                (k_ref[0, 0, k0:k0 + nk, :], vt_ref[0, 0, :, k0:k0 + nk], bias_ref[0, var])]

    scores = {i: _scores_t(paired_queries(CTX_LEN + i * nq, nq), block_pieces(i))
              for i in range(min(NA_AHEAD, len(blocks)))}
    for i in range(len(blocks)):
        q0 = CTX_LEN + i * nq
        if i + NA_AHEAD < len(blocks):
            scores[i + NA_AHEAD] = _scores_t(paired_queries(q0 + NA_AHEAD * nq, nq), block_pieces(i + NA_AHEAD))
        o_ref[0, q0:q0 + nq, :] = unpair(_softmax_pv_t(scores.pop(i), block_pieces(i), True), nq)


def _na(qdt, kd, vdt, bias_t, blocks, need_ctx):
    b, npair, t, pw = kd.shape
    nvar, nk, nq2 = bias_t.shape[1:]
    blk = pl.BlockSpec((1, 1, t, pw), lambda h, bb: (bb, h, 0, 0))
    blk_t = pl.BlockSpec((1, 1, pw, t), lambda h, bb: (bb, h, 0, 0))
    return pl.pallas_call(
        functools.partial(_na_kernel, need_ctx=need_ctx, blocks=blocks),
        grid=(npair, b),
        in_specs=[blk_t, blk, blk_t,
                  pl.BlockSpec((1, nvar, nk, nq2), lambda h, bb: (h, 0, 0, 0))],
        out_specs=pl.BlockSpec((1, t, pw), lambda h, bb: (bb, 0, h)),
        out_shape=jax.ShapeDtypeStruct((b, t, GROUP_WIDTH), F32),
        compiler_params=pltpu.CompilerParams(
            dimension_semantics=("parallel", "arbitrary"), vmem_limit_bytes=VMEM_LIMIT_BYTES),
        name="neighbourhood_attention",
    )(qdt, kd, vdt, bias_t)


def _na_bias_table(rpb, kh, win, variants):
    rows_q = NA_QROWS
    nrow, ncol = 2 * NA_KH - 1, 2 * NA_KW - 1
    col = np.arange(GRID_W)
    cstart = np.clip(col - NA_KW // 2, 0, GRID_W - NA_KW)
    col_in = (col[None, :] >= cstart[:, None]) & (col[None, :] < cstart[:, None] + NA_KW)
    col_idx = np.clip(col[None, :] - col[:, None], 1 - NA_KW, NA_KW - 1) + NA_KW - 1
    col_sel = (col_idx.T[None] == np.arange(ncol)[:, None, None]).astype(np.float32)
    row_sel, valid = [], []
    for ds, offs in variants:
        rel = ds + np.arange(win)[:, None] - np.arange(rows_q)[None, :]
        offs = np.asarray(offs)[None, :]
        valid.append((rel >= offs) & (rel < offs + kh))
        row_sel.append((rel + NA_KH - 1)[..., None] == np.arange(nrow))
    row_sel = np.stack(row_sel).astype(np.float32)
    keep = np.stack(valid)[:, :, None, None, :, None] & col_in.T[None, None, :, None, None, :]
    pairs = rpb.reshape(rpb.shape[0] // 2, 2, nrow, ncol).astype(F32)
    tbl = jnp.einsum('vxar,pgrc,cyq->pvxygaq', row_sel, pairs, col_sel, precision=lax.Precision.HIGHEST)
    tbl = jnp.where(keep[None], tbl * LOG2E, NEG_BIG)
    return tbl.reshape(rpb.shape[0] // 2, len(variants), win * GRID_W, 2 * rows_q * GRID_W)


def _outmlp_kernel(xc_ref, xl_ref, a_ref, b_ref, c_ref, d_ref, mlat_ref, mctx_ref, g2_ref, wo_ref, w1_ref,
                   w2_ref, gf_ref, o_ref, *, tile_off, final):
    j = pl.program_id(1) + tile_off
    mod = jnp.where(j == 0, mctx_ref[0], mlat_ref[0])
    cat = jnp.concatenate([a_ref[0], b_ref[0], c_ref[0], d_ref[0]], axis=-1).astype(BF16)
    x1 = jnp.where(j == 0, xc_ref[0], xl_ref[0]) + mod[2:3] * _dot(cat, wo_ref[...])
    ms = jnp.mean(x1 * x1, axis=-1, keepdims=True)
    h2 = ((x1 * lax.rsqrt(ms + EPS) * g2_ref[...]) * (1.0 + mod[4:5]) + mod[3:4]).astype(BF16)
    hid = w1_ref.shape[1]
    step = 1024
    acc = jnp.zeros(x1.shape, F32)
    nchunk = hid // step
    up = _dot(h2, w1_ref[:, 0:step])
    for c in range(nchunk):
        u = jnp.maximum(up, 0.0)
        if c + 1 < nchunk:
            up = _dot(h2, w1_ref[:, (c + 1) * step:(c + 2) * step])
        acc = acc + _dot((u * u).astype(BF16), w2_ref[c * step:(c + 1) * step, :])
    x2 = x1 + mod[5:6] * acc
    if final:
        ms2 = jnp.mean(x2 * x2, axis=-1, keepdims=True)
        x2 = x2 * lax.rsqrt(ms2 + EPS) * gf_ref[...]
    o_ref[0] = x2


def _outmlp(x_ctx, x_lat, a, bm, c, dd, mods_l, g2, wo, w1, w2, gf, final):
    b, t, _ = a.shape
    d = x_lat.shape[-1]
    tile_off = CTX_LEN // ROW_TILE if final else 0
    nt = t // ROW_TILE - tile_off
    ctx_row = mods_l.shape[0] - 1
    full2 = lambda bb, j: (0, 0)
    grp = pl.BlockSpec((1, ROW_TILE, GROUP_WIDTH), lambda bb, j: (bb, j + tile_off, 0))
    return pl.pallas_call(
        functools.partial(_outmlp_kernel, tile_off=tile_off, final=final),
        grid=(b, nt),
        in_specs=_token_sources(x_ctx, x_lat, tile_off) + [
            grp, grp, grp, grp,
            pl.BlockSpec((1, 6, d), lambda bb, j: (bb, 0, 0)),
            pl.BlockSpec((1, 6, d), lambda bb, j: (ctx_row, 0, 0)),
            pl.BlockSpec((1, d), full2),
            pl.BlockSpec(wo.shape, full2),
            pl.BlockSpec(w1.shape, full2),
            pl.BlockSpec(w2.shape, full2),
            pl.BlockSpec((1, d), full2),
        ],
        out_specs=pl.BlockSpec((1, ROW_TILE, d), lambda bb, j: (bb, j, 0)),
        out_shape=jax.ShapeDtypeStruct((b, nt * ROW_TILE, d), F32),
        compiler_params=pltpu.CompilerParams(
            dimension_semantics=("parallel", "arbitrary"), vmem_limit_bytes=VMEM_LIMIT_BYTES),
        name="out_projection_mlp",
    )(x_ctx, x_lat, a, bm, c, dd, mods_l, mods_l, g2, wo, w1, w2, gf)


def _pack_in_projection(w):
    gw, kvw = GROUP_WIDTH, KV_HEADS * HEAD_DIM
    g0 = 9 * gw
    c0 = g0 + 4 * GROUP_HEADS
    d0 = c0 + gw + 2 * kvw

    def pair_swapped(cols):
        n = cols.shape[1]
        return cols.reshape(cols.shape[0], n // 2, 2)[:, :, ::-1].reshape(cols.shape[0], n)

    q, k = w[:, c0:c0 + gw], w[:, c0 + gw:c0 + gw + kvw]
    parts = [w[:, 0:g0], w[:, g0:c0], jnp.zeros((w.shape[0], G_W - 4 * GROUP_HEADS), w.dtype),
             q, pair_swapped(q), k, pair_swapped(k), w[:, c0 + gw + kvw:d0], w[:, d0:d0 + 3 * gw]]
    packed = jnp.concatenate(parts, axis=1).astype(BF16)
    assert packed.shape[1] == IN_PACKED
    return packed


def _rope_tables(n_tokens):
    t = np.arange(n_tokens)
    row = (t // GRID_W).astype(np.float32)
    col = (t % GRID_W).astype(np.float32)
    axis_dims = HEAD_DIM // 2
    inv = np.power(np.float32(ROPE_THETA),
                   (-2.0 * np.arange(axis_dims // 2, dtype=np.float32) / axis_dims).astype(np.float32))
    ang = np.concatenate([row[:, None] * inv, col[:, None] * inv], axis=-1).astype(np.float32)
    cos = np.repeat(np.cos(ang).astype(np.float32), 2, axis=-1)
    sign = np.where(np.arange(HEAD_DIM) % 2 == 0, -1.0, 1.0).astype(np.float32)
    sin = np.repeat(np.sin(ang).astype(np.float32), 2, axis=-1) * sign
    cos = np.concatenate([np.ones((CTX_LEN, HEAD_DIM), np.float32), cos], axis=0)
    sin = np.concatenate([np.zeros((CTX_LEN, HEAD_DIM), np.float32), sin], axis=0)
    return jnp.asarray(np.tile(cos, (1, GROUP_HEADS))), jnp.asarray(np.tile(sin, (1, GROUP_HEADS)))


def kernel(x, c, ctx, c_ctx, w_mod, b_mod, norm1_g, norm2_g, w_in, hgrn_lb_logits, hgrn_norm_g,
           mlstm_gate_b, mlstm_norm_g, gqa_qnorm_g, gqa_knorm_g, na_rpb, w_out, w_mlp1, w_mlp2,
           final_norm_g):
    bsz, seq, d = x.shape
    depth = w_mod.shape[0]
    rows = seq // GRID_W

    n_c = bsz + 1
    pad = (-n_c) % 8
    cc = jnp.concatenate([c, c_ctx[None, :], jnp.zeros((pad, d), F32)], axis=0)
    mods = _modulation(cc, w_mod, b_mod)[:, :n_c].reshape(depth, n_c, 6, d)

    cos4, sin4 = _rope_tables(seq)
    swap = np.arange(HEAD_DIM) ^ 1

    x_ctx, x_lat = ctx, x
    out = None
    for l in range(depth):
        need_ctx = l < depth - 1
        w_packed = _pack_in_projection(w_in[l])
        gq = jnp.tile(gqa_qnorm_g[l], GROUP_HEADS)[None, :]
        gqs = jnp.tile(gqa_qnorm_g[l][swap], GROUP_HEADS)[None, :]
        gk = jnp.tile(gqa_knorm_g[l], KV_HEADS)[None, :]
        gks = jnp.tile(gqa_knorm_g[l][swap], KV_HEADS)[None, :]
        za, zb, zg, qc, kc, vc, qd, kd, vd = _inproj(
            x_ctx, x_lat, mods[l], norm1_g[l][None, :], w_packed, cos4, sin4, gq, gqs, gk, gks)

        a = _hgrn(za, hgrn_lb_logits, jnp.tile(hgrn_norm_g[l], GROUP_HEADS)[None, :], l, need_ctx)
        gate_b = jnp.concatenate([mlstm_gate_b[l], jnp.zeros((G_W - 4 * GROUP_HEADS,), F32)])[None, :]
        bm = _mlstm(zb, zg, gate_b, jnp.tile(mlstm_norm_g[l], GROUP_HEADS)[None, :], need_ctx)
        cg = _gqa(qc, kc, vc, need_ctx)
        kh, win, blocks, variants = _na_plan(rows)
        dg = _na(qd, kd, vd, _na_bias_table(na_rpb[l], kh, win, variants), blocks, need_ctx)

        res = _outmlp(x_ctx, x_lat, a, bm, cg, dg, mods[l], norm2_g[l][None, :], w_out[l].astype(BF16),
                      w_mlp1[l].astype(BF16), w_mlp2[l].astype(BF16), final_norm_g[None, :],
                      final=not need_ctx)
        if need_ctx:
            x_ctx = x_lat = res
        else:
            out = res
    return out
```

```python
import functools

import numpy as np
import jax
import jax.numpy as jnp
from jax import lax
from jax.experimental import pallas as pl
from jax.experimental.pallas import tpu as pltpu

F32 = jnp.float32
BF16 = jnp.bfloat16

D_MODEL = 1024
GRID_W = 64
CTX_LEN = 256
HEAD_DIM = 64
GROUP_HEADS = 4
GROUP_WIDTH = GROUP_HEADS * HEAD_DIM
KV_HEADS = 2
MLP_HIDDEN = 4 * D_MODEL
CHUNK = 64
SUB = 8
NA_KH = 8
NA_KW = 16
ROPE_THETA = 10000.0
EPS = 1e-6
NEG_BIG = -1e30
FORGET_FLOOR = 1e-20
ROW_TILE = 256
VMEM_LIMIT_BYTES = 56 * 1024 * 1024
LOG2E = 1.4426950408889634
Q_SCALE = HEAD_DIM ** -0.5 * LOG2E
NA_QROWS = 4
NA_AHEAD = 2
GQA_Q_TILES = 3
GQA_AHEAD = 2

A_OFF, A_W = 0, 5 * GROUP_WIDTH
B_OFF, B_W = A_OFF + A_W, 4 * GROUP_WIDTH
G_OFF, G_W = B_OFF + B_W, 128
C_OFF, C_W = G_OFF + G_W, 896
D_OFF, D_W = C_OFF + C_W, 3 * GROUP_WIDTH
IN_PACKED = D_OFF + D_W


def _dot(a, b):
    return jnp.dot(a, b, preferred_element_type=F32)


def _dot_nt(a, b):
    return lax.dot_general(a, b, (((1,), (1,)), ((), ())), preferred_element_type=F32)


def _dot_tn(a, b):
    return lax.dot_general(a, b, (((0,), (0,)), ((), ())), preferred_element_type=F32)


def _split3(x):
    hi = x.astype(BF16)
    r = x - hi.astype(F32)
    mid = r.astype(BF16)
    lo = (r - mid.astype(F32)).astype(BF16)
    return hi, mid, lo


def _route_l(m01, x):
    hi, mid, lo = _split3(x)
    return _dot(m01, hi) + _dot(m01, mid) + _dot(m01, lo)


def _route_r(x, m01):
    hi, mid, lo = _split3(x)
    return _dot(hi, m01) + _dot(mid, m01) + _dot(lo, m01)


def _route_r2(x, m01):
    hi = x.astype(BF16)
    mid = (x - hi.astype(F32)).astype(BF16)
    return _dot(hi, m01) + _dot(mid, m01)


def _sigmoid(x):
    return 1.0 / (1.0 + jnp.exp(-x))


def _log_sigmoid(x):
    return -(jnp.maximum(-x, 0.0) + jnp.log1p(jnp.exp(-jnp.abs(x))))


def _iota(shape, dim):
    return lax.broadcasted_iota(jnp.int32, shape, dim)


def _head_block_ones(n, m, dtype):
    return jnp.where((_iota((n, m), 0) >> 6) == (_iota((n, m), 1) >> 6), 1.0, 0.0).astype(dtype)


def _cumsum_matrix(rev):
    r, c = _iota((CHUNK, CHUNK), 0), _iota((CHUNK, CHUNK), 1)
    keep = (c >= r) if rev else (c <= r)
    return jnp.where(keep, 1.0, 0.0).astype(BF16)


def _head_rms(o, gain, ones_bd):
    ssq = _route_r2(o * o, ones_bd)
    return o * lax.rsqrt(ssq * (1.0 / HEAD_DIM) + EPS) * gain


def _mod_kernel(c_ref, w_ref, b_ref, o_ref):
    c = c_ref[...]
    s = c * _sigmoid(c)
    o_ref[0] = _dot(s.astype(BF16), w_ref[0].astype(BF16)) + b_ref[0]


def _modulation(cc, w_mod, b_mod):
    depth, d, n = w_mod.shape
    rows = cc.shape[0]
    tn = 1024
    return pl.pallas_call(
        _mod_kernel,
        grid=(depth, n // tn),
        in_specs=[
            pl.BlockSpec((rows, d), lambda l, j: (0, 0)),
            pl.BlockSpec((1, d, tn), lambda l, j: (l, 0, j)),
            pl.BlockSpec((1, 1, tn), lambda l, j: (l, 0, j)),
        ],
        out_specs=pl.BlockSpec((1, rows, tn), lambda l, j: (l, 0, j)),
        out_shape=jax.ShapeDtypeStruct((depth, rows, n), F32),
        compiler_params=pltpu.CompilerParams(
            dimension_semantics=("arbitrary", "arbitrary"), vmem_limit_bytes=VMEM_LIMIT_BYTES),
        name="adaln_modulation",
    )(cc, w_mod, b_mod.reshape(depth, 1, n))


def _read_tokens(refs, is_ctx_tile):
    if len(refs) == 1:
        return refs[0][0]
    return jnp.where(is_ctx_tile, refs[0][0], refs[1][0])


def _inproj_kernel(*refs):
    (mlat_ref, mctx_ref, g1_ref, w_ref, cos_ref, sin_ref, gq_ref, gqs_ref, gk_ref, gks_ref,
     za_ref, zb_ref, zg_ref, qc_ref, kc_ref, vc_ref, qd_ref, kd_ref, vd_ref) = refs[-19:]
    j = pl.program_id(1)
    mod = jnp.where(j == 0, mctx_ref[0], mlat_ref[0])
    x = _read_tokens(refs[:-19], j == 0)
    ms = jnp.mean(x * x, axis=-1, keepdims=True)
    y = x * lax.rsqrt(ms + EPS) * g1_ref[...]
    h = (y * (1.0 + mod[1:2]) + mod[0:1]).astype(BF16)

    zc = _dot(h, w_ref[:, C_OFF:C_OFF + C_W])
    zd = _dot(h, w_ref[:, D_OFF:D_OFF + D_W])
    za_ref[0] = _dot(h, w_ref[:, A_OFF:A_OFF + A_W])

    ones4 = _head_block_ones(GROUP_WIDTH, GROUP_WIDTH, BF16)
    cos, sin = cos_ref[...], sin_ref[...]
    q, qs = zc[:, 0:256], zc[:, 256:512]
    k, ks = zc[:, 512:640], zc[:, 640:768]
    ssq_q = _route_r2(q * q, ones4)
    ssq_k = _route_r2(k * k, ones4[0:128, 0:128])
    zb_ref[0] = _dot(h, w_ref[:, B_OFF:B_OFF + B_W])
    zg_ref[0] = _dot(h, w_ref[:, G_OFF:G_OFF + G_W])
    rq = lax.rsqrt(ssq_q * (1.0 / HEAD_DIM) + EPS)
    qrt = ((rq * (q * gq_ref[...] * cos + qs * gqs_ref[...] * sin)) * Q_SCALE).T
    rk = lax.rsqrt(ssq_k * (1.0 / HEAD_DIM) + EPS)
    kr = (rk * (k * gk_ref[...] * cos[:, 0:128] + ks * gks_ref[...] * sin[:, 0:128])).astype(BF16)
    vvt = zc[:, 768:896].T
    for hh in range(GROUP_HEADS):
        qc_ref[0, hh] = qrt[hh * HEAD_DIM:(hh + 1) * HEAD_DIM].astype(BF16)
    for hh in range(KV_HEADS):
        kc_ref[0, hh] = kr[:, hh * HEAD_DIM:(hh + 1) * HEAD_DIM]
        vc_ref[0, hh] = vvt[hh * HEAD_DIM:(hh + 1) * HEAD_DIM].astype(BF16)

    qdt = (zd[:, 0:256] * Q_SCALE).T
    kdb = zd[:, 256:512].astype(BF16)
    vdt = zd[:, 512:768].T
    for pp in range(GROUP_HEADS // 2):
        lo, hi = pp * 2 * HEAD_DIM, (pp + 1) * 2 * HEAD_DIM
        qd_ref[0, pp] = qdt[lo:hi].astype(BF16)
        kd_ref[0, pp] = kdb[:, lo:hi]
        vd_ref[0, pp] = vdt[lo:hi].astype(BF16)


def _token_sources(x_ctx, x_lat, tile_off=0):
    assert CTX_LEN == ROW_TILE
    d = x_lat.shape[-1]
    if x_lat is x_ctx:
        return [x_lat], [pl.BlockSpec((1, ROW_TILE, d), lambda bb, j: (bb, j + tile_off, 0))]
    return [x_ctx, x_lat], [pl.BlockSpec((1, ROW_TILE, d), lambda bb, j: (bb, 0, 0)),
                            pl.BlockSpec((1, ROW_TILE, d), lambda bb, j: (bb, jnp.maximum(j + tile_off, 1) - 1, 0))]


def _inproj(x_ctx, x_lat, mods_l, g1, w_packed, cos4, sin4, gq, gqs, gk, gks):
    b, _, d = x_lat.shape
    t = x_lat.shape[1] if x_lat is x_ctx else x_ctx.shape[1] + x_lat.shape[1]
    nt = t // ROW_TILE
    ctx_row = mods_l.shape[0] - 1
    full2 = lambda bb, j: (0, 0)
    headed = lambda nh, w=HEAD_DIM: pl.BlockSpec((1, nh, ROW_TILE, w), lambda bb, j: (bb, 0, j, 0))
    headed_t = lambda nh, w=HEAD_DIM: pl.BlockSpec((1, nh, w, ROW_TILE), lambda bb, j: (bb, 0, 0, j))
    flat = lambda w: pl.BlockSpec((1, ROW_TILE, w), lambda bb, j: (bb, j, 0))
    hshape = lambda nh, w=HEAD_DIM: jax.ShapeDtypeStruct((b, nh, t, w), BF16)
    hshape_t = lambda nh, w=HEAD_DIM: jax.ShapeDtypeStruct((b, nh, w, t), BF16)
    fshape = lambda w: jax.ShapeDtypeStruct((b, t, w), F32)
    pw = 2 * HEAD_DIM
    tokens, token_specs = _token_sources(x_ctx, x_lat)
    return pl.pallas_call(
        _inproj_kernel,
        grid=(b, nt),
        in_specs=token_specs + [
            pl.BlockSpec((1, 6, d), lambda bb, j: (bb, 0, 0)),
            pl.BlockSpec((1, 6, d), lambda bb, j: (ctx_row, 0, 0)),
            pl.BlockSpec((1, d), full2),
            pl.BlockSpec((d, IN_PACKED), full2),
            pl.BlockSpec((ROW_TILE, GROUP_WIDTH), lambda bb, j: (j, 0)),
            pl.BlockSpec((ROW_TILE, GROUP_WIDTH), lambda bb, j: (j, 0)),
            pl.BlockSpec((1, GROUP_WIDTH), full2),
            pl.BlockSpec((1, GROUP_WIDTH), full2),
            pl.BlockSpec((1, 128), full2),
            pl.BlockSpec((1, 128), full2),
        ],
        out_specs=[flat(A_W), flat(B_W), flat(G_W), headed_t(4), headed(2), headed_t(2),
                   headed_t(2, pw), headed(2, pw), headed_t(2, pw)],
        out_shape=[fshape(A_W), fshape(B_W), fshape(G_W), hshape_t(4), hshape(2), hshape_t(2),
                   hshape_t(2, pw), hshape(2, pw), hshape_t(2, pw)],
        compiler_params=pltpu.CompilerParams(
            dimension_semantics=("parallel", "arbitrary"), vmem_limit_bytes=VMEM_LIMIT_BYTES),
        name="in_projection",
    )(*tokens, mods_l, mods_l, g1, w_packed, cos4, sin4, gq, gqs, gk, gks)


def _hgrn_keys_decay(za_ref, row0, lb, rev):
    fcol = 4 * GROUP_WIDTH if rev else 3 * GROUP_WIDTH
    zf = za_ref[0, pl.ds(row0, ROW_TILE), fcol:fcol + GROUP_WIDTH]
    sg = _sigmoid(zf)
    g = jnp.log(jnp.maximum(lb + (1.0 - lb) * sg, FORGET_FLOOR))
    return (1.0 - lb) * (1.0 - sg), g


def _hgrn_tile_gates(za_ref, row0, lb, rev, cmat4):
    k, g = _hgrn_keys_decay(za_ref, row0, lb, rev)
    return k, _route_l(cmat4, g)


HGRN_SCALES = (1, 2, 4, 8, 16, 32)


def _hgrn_scale_consts(rev):
    t = _iota((CHUNK, GROUP_WIDTH), 0)
    s = _iota((CHUNK, GROUP_WIDTH), 1) & (HEAD_DIM - 1)
    out = []
    for c in HGRN_SCALES:
        sh = c.bit_length()
        t_hi = (t & (2 * c - 1)) >= c
        s_hi = (s & (2 * c - 1)) >= c
        isq = jnp.logical_not(t_hi) if rev else t_hi
        iskey = s_hi if rev else jnp.logical_not(s_hi)
        pair = jnp.where(((t >> sh) == (s >> sh)) & isq & iskey, 1.0, 0.0).astype(F32)
        out.append((isq, jnp.where(isq, 1.0, -1.0).astype(F32), pair))
    return out, jnp.where(t == s, 1.0, 0.0).astype(F32)


def _hgrn_boundary_rows(b, c, rev):
    off = c if rev else c - 1
    if 2 * c >= SUB:
        pieces = [jnp.broadcast_to(b[lo + off:lo + off + 1], (2 * c, b.shape[1]))
                  for lo in range(0, CHUNK, 2 * c)]
        return pieces[0] if len(pieces) == 1 else jnp.concatenate(pieces, axis=0)
    pos = _iota(b.shape, 0) & (2 * c - 1)
    out = b
    for j in range(2 * c):
        if j != off:
            out = jnp.where(pos == j, pltpu.roll(b, (j - off) % CHUNK, 0), out)
    return out


def _hgrn_chunk_local(za_ref, row0, k, b, rev, want_out, consts):
    ones_bd, bmask, scales, diag = consts
    v = za_ref[0, pl.ds(row0, CHUNK), GROUP_WIDTH:2 * GROUP_WIDTH]
    b_end = b[0:1] if rev else b[CHUNK - 1:CHUNK]
    vb = v.astype(BF16)
    upd = _dot_tn(vb, (k * jnp.exp(b_end - b)).astype(BF16)) * bmask
    if not want_out:
        return jnp.exp(b_end), upd, None, None
    zq = za_ref[0, pl.ds(row0, CHUNK), 0:GROUP_WIDTH]
    q = zq * _sigmoid(zq) * (HEAD_DIM ** -0.5)
    stack = lambda x: jnp.concatenate([x] * GROUP_HEADS, axis=0) * ones_bd
    a = _dot_nt(q.astype(BF16), stack(k.astype(BF16))) * diag
    for c, (isq, sign, pair) in zip(HGRN_SCALES, scales):
        d = b - _hgrn_boundary_rows(b, c, rev)
        x = (jnp.where(isq, q, k) * jnp.exp(d * sign)).astype(BF16)
        a = a + _dot_nt(x, stack(x)) * pair
    return jnp.exp(b_end), upd, (q * jnp.exp(b)).astype(BF16), _dot(a.astype(BF16), stack(vb))


def _hgrn_chunk_carry(o_ref, row0, st, local):
    decay, upd, qe, o_intra = local
    if qe is not None:
        o_ref[0, pl.ds(row0, CHUNK), :] += o_intra + _dot_nt(qe, st.astype(BF16))
    return st * decay + upd


def _hgrn_kernel(za_ref, lg_ref, gain_ref, o_ref, stf_ref, stb_ref, gates_ref, *, layer, need_ctx):
    t = za_ref.shape[1]
    n_tiles = t // ROW_TILE
    per_tile = ROW_TILE // CHUNK
    lg = lg_ref[...]
    depth = lg.shape[0]
    mx = lg[0]
    for i in range(1, depth):
        mx = jnp.maximum(mx, lg[i])
    es = [jnp.exp(lg[i] - mx) for i in range(depth)]
    tot = es[0]
    for i in range(1, depth):
        tot = tot + es[i]
    sm = [e / tot for e in es]
    cs = sm[0]
    for i in range(1, layer + 1):
        cs = cs + sm[i]
    lbs = cs - sm[0]
    lb_f, lb_b = lbs[0:1], lbs[1:2]

    ones_bd = _head_block_ones(GROUP_WIDTH, GROUP_WIDTH, BF16)
    bmask = _head_block_ones(GROUP_WIDTH, GROUP_WIDTH, F32)
    consts_f = (ones_bd, bmask) + _hgrn_scale_consts(False)
    consts_b = (ones_bd, bmask) + _hgrn_scale_consts(True)
    r4, c4 = _iota((ROW_TILE, ROW_TILE), 0), _iota((ROW_TILE, ROW_TILE), 1)
    same_chunk = (r4 >> 6) == (c4 >> 6)
    cum_f = jnp.where(same_chunk & (c4 <= r4), 1.0, 0.0).astype(BF16)
    cum_b = jnp.where(same_chunk & (c4 >= r4), 1.0, 0.0).astype(BF16)

    stf_ref[...] = jnp.zeros_like(stf_ref)
    stb_ref[...] = jnp.zeros_like(stb_ref)
    o_ref[...] = jnp.zeros_like(o_ref)

    def tile_step(row_f, row_b, want, gates, next_rows):
        k_f, b_f, k_b, b_b = gates
        kn_f, gn_f = _hgrn_keys_decay(za_ref, next_rows[0], lb_f, False)
        kn_b, gn_b = _hgrn_keys_decay(za_ref, next_rows[1], lb_b, True)
        lo_f = [c * CHUNK for c in range(per_tile)]
        lo_b = [(per_tile - 1 - c) * CHUNK for c in range(per_tile)]
        loc_f = [_hgrn_chunk_local(za_ref, row_f + lo, k_f[lo:lo + CHUNK], b_f[lo:lo + CHUNK], False, want,
                                   consts_f) for lo in lo_f]
        bn_f = _route_l(cum_f, gn_f)
        loc_b = [_hgrn_chunk_local(za_ref, row_b + lo, k_b[lo:lo + CHUNK], b_b[lo:lo + CHUNK], True, want,
                                   consts_b) for lo in lo_b]
        bn_b = _route_l(cum_b, gn_b)
        st_f, st_b = stf_ref[...], stb_ref[...]
        for c in range(per_tile):
            st_f = _hgrn_chunk_carry(o_ref, row_f + lo_f[c], st_f, loc_f[c])
            st_b = _hgrn_chunk_carry(o_ref, row_b + lo_b[c], st_b, loc_b[c])
        stf_ref[...] = st_f
        stb_ref[...] = st_b
        return kn_f, bn_f, kn_b, bn_b

    n_ctx_tiles = CTX_LEN // ROW_TILE
    lat_rows = lambda i: (pl.multiple_of(jnp.minimum(n_ctx_tiles + i, n_tiles - 1) * ROW_TILE, ROW_TILE),
                          pl.multiple_of(jnp.maximum(n_tiles - 1 - i, 0) * ROW_TILE, ROW_TILE))

    def park(gates):
        for n, g in enumerate(gates):
            gates_ref[n] = g

    gates = (_hgrn_tile_gates(za_ref, 0, lb_f, False, cum_f)
             + _hgrn_tile_gates(za_ref, (n_ctx_tiles - 1) * ROW_TILE, lb_b, True, cum_b))
    for i in range(n_ctx_tiles):
        if i + 1 < n_ctx_tiles:
            next_rows = ((i + 1) * ROW_TILE, (n_ctx_tiles - 2 - i) * ROW_TILE)
        else:
            next_rows = lat_rows(0)
        gates = tile_step(i * ROW_TILE, (n_ctx_tiles - 1 - i) * ROW_TILE, need_ctx, gates, next_rows)
    park(gates)

    def lat_body(i, carry):
        row_f, row_b = lat_rows(i)
        park(tile_step(row_f, row_b, True, tuple(gates_ref[n] for n in range(4)), lat_rows(i + 1)))
        return carry

    lax.fori_loop(0, n_tiles - n_ctx_tiles, lat_body, 0)

    gain = gain_ref[...]

    def fin_body(i, carry):
        r0 = pl.multiple_of(i * ROW_TILE, ROW_TILE)
        o = o_ref[0, pl.ds(r0, ROW_TILE), :]
        zo = za_ref[0, pl.ds(r0, ROW_TILE), 2 * GROUP_WIDTH:3 * GROUP_WIDTH]
        o_ref[0, pl.ds(r0, ROW_TILE), :] = _head_rms(o, gain, ones_bd) * (zo * _sigmoid(zo))
        return carry

    lax.fori_loop(0, t // ROW_TILE, fin_body, 0)


def _hgrn(za, lb_logits, gain4, layer, need_ctx):
    b, t, _ = za.shape
    depth = lb_logits.shape[0]
    return pl.pallas_call(
        functools.partial(_hgrn_kernel, layer=layer, need_ctx=need_ctx),
        grid=(b,),
        in_specs=[
            pl.BlockSpec((1, t, A_W), lambda bb: (bb, 0, 0)),
            pl.BlockSpec((depth, 2, GROUP_WIDTH), lambda bb: (0, 0, 0)),
            pl.BlockSpec((1, GROUP_WIDTH), lambda bb: (0, 0)),
        ],
        out_specs=pl.BlockSpec((1, t, GROUP_WIDTH), lambda bb: (bb, 0, 0)),
        out_shape=jax.ShapeDtypeStruct((b, t, GROUP_WIDTH), F32),
        scratch_shapes=[pltpu.VMEM((GROUP_WIDTH, GROUP_WIDTH), F32),
                        pltpu.VMEM((GROUP_WIDTH, GROUP_WIDTH), F32),
                        pltpu.VMEM((4, ROW_TILE, GROUP_WIDTH), F32)],
        compiler_params=pltpu.CompilerParams(
            dimension_semantics=("parallel",), vmem_limit_bytes=VMEM_LIMIT_BYTES),
        name="hgrn2_scan",
    )(za, lb_logits, gain4)


def _mlstm_gate_routes(zg_ref, row0, gate_b, consts):
    e_i, e_f = consts[1:3]
    hi, mid, lo = _split3(zg_ref[0, pl.ds(row0, ROW_TILE), :] + gate_b)
    ig = _dot(hi, e_i) + _dot(mid, e_i) + _dot(lo, e_i)
    return ig, _log_sigmoid(_dot(hi, e_f) + _dot(mid, e_f) + _dot(lo, e_f))


def _mlstm_tile_gates(zg_ref, row0, gate_b, consts):
    ig, lf = _mlstm_gate_routes(zg_ref, row0, gate_b, consts)
    return ig, _route_l(consts[0], lf)


def _mlstm_chunk_local(zb_ref, row0, ig, bc, rev, want_out, consts):
    ones_bd, bmask, diag, causal = consts[3:7]
    k = zb_ref[0, pl.ds(row0, CHUNK), GROUP_WIDTH:2 * GROUP_WIDTH] * (HEAD_DIM ** -0.5)
    v = zb_ref[0, pl.ds(row0, CHUNK), 2 * GROUP_WIDTH:3 * GROUP_WIDTH]
    b_end = bc[0:1] if rev else bc[CHUNK - 1:CHUNK]
    vb = v.astype(BF16)
    gs = b_end - bc + ig
    g_max = jnp.max(gs, axis=0, keepdims=True)
    kw = k * jnp.exp(gs - g_max)
    loc = dict(b_end=b_end, g_max=g_max, bc=bc, upd=_dot_tn(vb, kw.astype(BF16)) * bmask,
               nupd=jnp.sum(kw, axis=0, keepdims=True), q=None)
    if want_out:
        q = zb_ref[0, pl.ds(row0, CHUNK), 0:GROUP_WIDTH]
        qb = q.astype(BF16)
        rowv = jnp.sum((ig - bc) * diag, axis=0, keepdims=True)
        dlog = jnp.where(causal, bc + rowv, NEG_BIG)
        segs = []
        for hh in range(GROUP_HEADS):
            mh = jnp.max(dlog[:, hh * HEAD_DIM:(hh + 1) * HEAD_DIM], axis=-1, keepdims=True)
            segs.append(jnp.broadcast_to(mh, (CHUNK, HEAD_DIM)))
        m_loc = jnp.concatenate(segs, axis=-1)
        kk = jnp.concatenate([k.astype(BF16)] * GROUP_HEADS, axis=0) * ones_bd
        vv = jnp.concatenate([vb] * GROUP_HEADS, axis=0) * ones_bd
        pb = (jnp.exp(dlog - m_loc) * _dot_nt(qb, kk)).astype(BF16)
        loc.update(q=q, qb=qb, m_loc=m_loc, num_loc=_dot(pb, vv), den_loc=_dot(pb, ones_bd))
    return loc


def _mlstm_chunk_carry(o_ref, row0, state, loc, ones_bd):
    ct, n_prev, m_prev = state
    if loc["q"] is not None:
        qc = _dot_nt(loc["qb"], ct.astype(BF16))
        qn = _dot((loc["q"] * n_prev).astype(BF16), ones_bd)
        a = loc["bc"] + m_prev
        m_t = jnp.maximum(a, loc["m_loc"])
        w_in = jnp.exp(a - m_t)
        w_loc = jnp.exp(loc["m_loc"] - m_t)
        num = w_in * qc + w_loc * loc["num_loc"]
        den = w_in * qn + w_loc * loc["den_loc"]
        o_ref[0, pl.ds(row0, CHUNK), :] += num / jnp.maximum(jnp.abs(den), jnp.exp(-m_t))
    m_new = jnp.maximum(loc["b_end"] + m_prev, loc["g_max"])
    w_old = jnp.exp(loc["b_end"] + m_prev - m_new)
    w_upd = jnp.exp(loc["g_max"] - m_new)
    return ct * w_old + loc["upd"] * w_upd, n_prev * w_old + loc["nupd"] * w_upd, m_new


def _mlstm_kernel(zb_ref, zg_ref, gb_ref, gain_ref, o_ref, ctf_ref, ctb_ref, nf_ref, nb_ref,
                  mf_ref, mb_ref, gates_ref, *, need_ctx):
    t = zb_ref.shape[1]
    n_tiles = t // ROW_TILE
    per_tile = ROW_TILE // CHUNK
    ones_bd = _head_block_ones(GROUP_WIDTH, GROUP_WIDTH, BF16)
    bmask = _head_block_ones(GROUP_WIDTH, GROUP_WIDTH, F32)
    gcol = _iota((128, GROUP_WIDTH), 0)
    ghead = _iota((128, GROUP_WIDTH), 1) >> 6
    route = lambda base: jnp.where(gcol == base + ghead, 1.0, 0.0).astype(BF16)
    tt = _iota((CHUNK, GROUP_WIDTH), 0)
    ss = _iota((CHUNK, GROUP_WIDTH), 1) & (HEAD_DIM - 1)
    diag = jnp.where(tt == ss, 1.0, 0.0).astype(F32)
    r4, c4 = _iota((ROW_TILE, ROW_TILE), 0), _iota((ROW_TILE, ROW_TILE), 1)
    same_chunk = (r4 >> 6) == (c4 >> 6)
    cum4 = lambda rev: jnp.where(same_chunk & ((c4 >= r4) if rev else (c4 <= r4)), 1.0, 0.0).astype(BF16)
    cf = (cum4(False), route(0), route(2 * GROUP_HEADS), ones_bd, bmask, diag, ss <= tt)
    cb = (cum4(True), route(GROUP_HEADS), route(3 * GROUP_HEADS), ones_bd, bmask, diag, ss >= tt)
    gate_b = gb_ref[...]

    for r in (ctf_ref, ctb_ref, nf_ref, nb_ref, mf_ref, mb_ref, o_ref):
        r[...] = jnp.zeros_like(r)

    def tile_gates(row_f, row_b):
        return (_mlstm_tile_gates(zg_ref, row_f, gate_b, cf) + _mlstm_tile_gates(zg_ref, row_b, gate_b, cb))

    def tile_step(row_f, row_b, want, gates, next_rows=None):
        ig_f, bc_f, ig_b, bc_b = gates
        lo_f = [c * CHUNK for c in range(per_tile)]
        lo_b = [(per_tile - 1 - c) * CHUNK for c in range(per_tile)]
        st_f = (ctf_ref[...], nf_ref[0:1], mf_ref[0:1])
        st_b = (ctb_ref[...], nb_ref[0:1], mb_ref[0:1])
        nxt = None
        for c in range(per_tile):
            if next_rows is not None and c == 1:
                nxt = [_mlstm_gate_routes(zg_ref, r, gate_b, cc) for r, cc in zip(next_rows, (cf, cb))]
            if next_rows is not None and c == 3:
                nxt = (nxt[0][0], _route_l(cf[0], nxt[0][1]), nxt[1][0], _route_l(cb[0], nxt[1][1]))
            lo = lo_f[c]
            loc = _mlstm_chunk_local(zb_ref, row_f + lo, ig_f[lo:lo + CHUNK], bc_f[lo:lo + CHUNK], False, want, cf)
            st_f = _mlstm_chunk_carry(o_ref, row_f + lo, st_f, loc, ones_bd)
            lo = lo_b[c]
            loc = _mlstm_chunk_local(zb_ref, row_b + lo, ig_b[lo:lo + CHUNK], bc_b[lo:lo + CHUNK], True, want, cb)
            st_b = _mlstm_chunk_carry(o_ref, row_b + lo, st_b, loc, ones_bd)
        for refs, st in (((ctf_ref, nf_ref, mf_ref), st_f), ((ctb_ref, nb_ref, mb_ref), st_b)):
            refs[0][...] = st[0]
            refs[1][...] = jnp.broadcast_to(st[1], refs[1].shape)
            refs[2][...] = jnp.broadcast_to(st[2], refs[2].shape)
        return nxt

    n_ctx_tiles = CTX_LEN // ROW_TILE
    n_lat = n_tiles - n_ctx_tiles
    lat_rows = lambda i: (pl.multiple_of(jnp.minimum(n_ctx_tiles + i, n_tiles - 1) * ROW_TILE, ROW_TILE),
                          pl.multiple_of(jnp.maximum(n_tiles - 1 - i, 0) * ROW_TILE, ROW_TILE))

    def park(gates):
        for n, g in enumerate(gates):
            gates_ref[n] = g

    gates = tile_gates(0, (n_ctx_tiles - 1) * ROW_TILE)
    for i in range(n_ctx_tiles):
        if i + 1 < n_ctx_tiles:
            next_rows = ((i + 1) * ROW_TILE, (n_ctx_tiles - 2 - i) * ROW_TILE)
        else:
            next_rows = lat_rows(0)
        gates = tile_step(i * ROW_TILE, (n_ctx_tiles - 1 - i) * ROW_TILE, need_ctx, gates, next_rows)
    park(gates)

    def lat_body(i, carry):
        row_f, row_b = lat_rows(i)
        park(tile_step(row_f, row_b, True, tuple(gates_ref[n] for n in range(4)), lat_rows(i + 1)))
        return carry

    lax.fori_loop(0, n_lat, lat_body, 0)

    gain = gain_ref[...]

    def fin_body(i, carry):
        r0 = pl.multiple_of(i * ROW_TILE, ROW_TILE)
        o = o_ref[0, pl.ds(r0, ROW_TILE), :]
        zo = zb_ref[0, pl.ds(r0, ROW_TILE), 3 * GROUP_WIDTH:4 * GROUP_WIDTH]
        o_ref[0, pl.ds(r0, ROW_TILE), :] = _head_rms(o, gain, ones_bd) * _sigmoid(zo)
        return carry

    lax.fori_loop(0, t // ROW_TILE, fin_body, 0)


def _mlstm(zb, zg, gate_b128, gain4, need_ctx):
    b, t, _ = zb.shape
    return pl.pallas_call(
        functools.partial(_mlstm_kernel, need_ctx=need_ctx),
        grid=(b,),
        in_specs=[
            pl.BlockSpec((1, t, B_W), lambda bb: (bb, 0, 0)),
            pl.BlockSpec((1, t, G_W), lambda bb: (bb, 0, 0)),
            pl.BlockSpec((1, G_W), lambda bb: (0, 0)),
            pl.BlockSpec((1, GROUP_WIDTH), lambda bb: (0, 0)),
        ],
        out_specs=pl.BlockSpec((1, t, GROUP_WIDTH), lambda bb: (bb, 0, 0)),
        out_shape=jax.ShapeDtypeStruct((b, t, GROUP_WIDTH), F32),
        scratch_shapes=[pltpu.VMEM((GROUP_WIDTH, GROUP_WIDTH), F32),
                        pltpu.VMEM((GROUP_WIDTH, GROUP_WIDTH), F32),
                        pltpu.VMEM((SUB, GROUP_WIDTH), F32), pltpu.VMEM((SUB, GROUP_WIDTH), F32),
                        pltpu.VMEM((SUB, GROUP_WIDTH), F32), pltpu.VMEM((SUB, GROUP_WIDTH), F32),
                        pltpu.VMEM((4, ROW_TILE, GROUP_WIDTH), F32)],
        compiler_params=pltpu.CompilerParams(
            dimension_semantics=("parallel",), vmem_limit_bytes=VMEM_LIMIT_BYTES),
        name="mlstm_scan",
    )(zb, zg, gate_b128, gain4)


def _attend_t(qt, pieces):
    return _softmax_pv_t(_scores_t(qt, pieces), pieces, True)


def _scores_t(qt, pieces):
    s_list = []
    for k, _, bias in pieces:
        s = _dot(k, qt)
        s_list.append(s if bias is None else s + bias)
    return s_list


def _softmax_pv_t(s_list, pieces, mxu_denominator):
    m = jnp.max(s_list[0], axis=0, keepdims=True)
    for s in s_list[1:]:
        m = jnp.maximum(m, jnp.max(s, axis=0, keepdims=True))
    if mxu_denominator:
        pad = 2 * SUB
        acc = None
        for s, (_, vt, _) in zip(s_list, pieces):
            vt1 = jnp.concatenate([vt, jnp.ones((pad, vt.shape[1]), vt.dtype)], axis=0)
            pv = _dot(vt1, jnp.exp2(s - m).astype(BF16))
            acc = pv if acc is None else acc + pv
        hd = acc.shape[0] - pad
        return acc[0:hd] / acc[hd:hd + 1]
    l = acc = None
    for s, (_, vt, _) in zip(s_list, pieces):
        p = jnp.exp2(s - m)
        ls = jnp.sum(p, axis=0, keepdims=True)
        pv = _dot(vt, p.astype(BF16))
        l = ls if l is None else l + ls
        acc = pv if acc is None else acc + pv
    return acc / l


def _gqa_kernel(q_ref, k_ref, vt_ref, o_ref, *, need_ctx):
    j = pl.program_id(2)
    t = k_ref.shape[2]
    g = q_ref.shape[1]
    nsub = q_ref.shape[3] // ROW_TILE

    def queries(sub):
        return jnp.concatenate([q_ref[0, i, :, sub * ROW_TILE:(sub + 1) * ROW_TILE] for i in range(g)], axis=1)

    def run(has_ctx_tile):
        units = []
        for sub in range(nsub):
            if has_ctx_tile and sub < CTX_LEN // ROW_TILE:
                if need_ctx:
                    units.append((sub, [(k_ref[0, 0, 0:CTX_LEN, :], vt_ref[0, 0, :, 0:CTX_LEN], None)]))
                else:
                    o_ref[0, sub * ROW_TILE:(sub + 1) * ROW_TILE, :] = jnp.zeros((ROW_TILE, g * HEAD_DIM), F32)
            else:
                units.append((sub, [(k_ref[0, 0], vt_ref[0, 0], None)]))
        scores = {n: _scores_t(queries(units[n][0]), units[n][1]) for n in range(min(GQA_AHEAD, len(units)))}
        for n, (sub, pieces) in enumerate(units):
            if n + GQA_AHEAD < len(units):
                scores[n + GQA_AHEAD] = _scores_t(queries(units[n + GQA_AHEAD][0]), units[n + GQA_AHEAD][1])
            ot = _softmax_pv_t(scores.pop(n), pieces, False)
            o_ref[0, sub * ROW_TILE:(sub + 1) * ROW_TILE, :] = jnp.concatenate(
                [ot[:, i * ROW_TILE:(i + 1) * ROW_TILE] for i in range(g)], axis=0).T

    if nsub * ROW_TILE == t:
        run(True)
    else:
        @pl.when(j == 0)
        def _():
            run(True)

        @pl.when(j > 0)
        def _():
            run(False)


def _gqa(qct, kc, vct, need_ctx):
    b, _, t, _ = kc.shape
    g = GROUP_HEADS // KV_HEADS
    qt = GQA_Q_TILES * ROW_TILE
    assert t % qt == 0
    return pl.pallas_call(
        functools.partial(_gqa_kernel, need_ctx=need_ctx),
        grid=(b, KV_HEADS, t // qt),
        in_specs=[
            pl.BlockSpec((1, g, HEAD_DIM, qt), lambda bb, h, j: (bb, h, 0, j)),
            pl.BlockSpec((1, 1, t, HEAD_DIM), lambda bb, h, j: (bb, h, 0, 0)),
            pl.BlockSpec((1, 1, HEAD_DIM, t), lambda bb, h, j: (bb, h, 0, 0)),
        ],
        out_specs=pl.BlockSpec((1, qt, g * HEAD_DIM), lambda bb, h, j: (bb, j, h)),
        out_shape=jax.ShapeDtypeStruct((b, t, GROUP_WIDTH), F32),
        compiler_params=pltpu.CompilerParams(
            dimension_semantics=("parallel", "parallel", "arbitrary"), vmem_limit_bytes=VMEM_LIMIT_BYTES),
        name="gqa_attention",
    )(qct, kc, vct)


def _na_plan(rows):
    kh = min(NA_KH, rows)
    win = min(kh + NA_QROWS, rows)
    assert rows % NA_QROWS == 0 and (win * GRID_W) % 128 == 0
    blocks, variants = [], []
    for i in range(rows // NA_QROWS):
        qrows = np.arange(NA_QROWS * i, NA_QROWS * (i + 1))
        r0 = np.clip(qrows - kh // 2, 0, rows - kh)
        start = int(np.clip(NA_QROWS * i - kh // 2, 0, rows - win))
        assert start % 2 == 0 and start <= r0.min() and r0.max() + kh <= start + win
        key = (start - NA_QROWS * i, tuple(int(v) for v in r0 - qrows))
        if key not in variants:
            variants.append(key)
        blocks.append((start, variants.index(key)))
    return kh, win, tuple(blocks), variants


def _na_kernel(q_ref, k_ref, vt_ref, bias_ref, o_ref, *, need_ctx, blocks):
    nq = NA_QROWS * GRID_W
    nk = bias_ref.shape[2]
    kctx = k_ref[0, 0, 0:CTX_LEN, :]
    vtctx = vt_ref[0, 0, :, 0:CTX_LEN]

    def paired_queries(lo, n):
        q = q_ref[0, 0, :, lo:lo + n]
        z = jnp.zeros((HEAD_DIM, n), q.dtype)
        return jnp.concatenate([jnp.concatenate([q[0:HEAD_DIM], z], axis=1),
                                jnp.concatenate([z, q[HEAD_DIM:2 * HEAD_DIM]], axis=1)], axis=0)

    def unpair(ot, n):
        return jnp.concatenate([ot[0:HEAD_DIM, 0:n], ot[HEAD_DIM:2 * HEAD_DIM, n:2 * n]], axis=0).T

    if need_ctx:
        ot = _attend_t(paired_queries(0, CTX_LEN), [(kctx, vtctx, None)])
        o_ref[0, 0:CTX_LEN, :] = unpair(ot, CTX_LEN)
    else:
        o_ref[0, 0:CTX_LEN, :] = jnp.zeros((CTX_LEN, 2 * HEAD_DIM), F32)

    def block_pieces(i):
        start, var = blocks[i]
        k0 = CTX_LEN + start * GRID_W
        return [(kctx, vtctx, None),
                (k_ref[0, 0, k0:k0 + nk, :], vt_ref[0, 0, :, k0:k0 + nk], bias_ref[0, var])]

    scores = {i: _scores_t(paired_queries(CTX_LEN + i * nq, nq), block_pieces(i))
              for i in range(min(NA_AHEAD, len(blocks)))}
    for i in range(len(blocks)):
        q0 = CTX_LEN + i * nq
        if i + NA_AHEAD < len(blocks):
            scores[i + NA_AHEAD] = _scores_t(paired_queries(q0 + NA_AHEAD * nq, nq), block_pieces(i + NA_AHEAD))
        o_ref[0, q0:q0 + nq, :] = unpair(_softmax_pv_t(scores.pop(i), block_pieces(i), True), nq)


def _na(qdt, kd, vdt, bias_t, blocks, need_ctx):
    b, npair, t, pw = kd.shape
    nvar, nk, nq2 = bias_t.shape[1:]
    blk = pl.BlockSpec((1, 1, t, pw), lambda h, bb: (bb, h, 0, 0))
    blk_t = pl.BlockSpec((1, 1, pw, t), lambda h, bb: (bb, h, 0, 0))
    return pl.pallas_call(
        functools.partial(_na_kernel, need_ctx=need_ctx, blocks=blocks),
        grid=(npair, b),
        in_specs=[blk_t, blk, blk_t,
                  pl.BlockSpec((1, nvar, nk, nq2), lambda h, bb: (h, 0, 0, 0))],
        out_specs=pl.BlockSpec((1, t, pw), lambda h, bb: (bb, 0, h)),
        out_shape=jax.ShapeDtypeStruct((b, t, GROUP_WIDTH), F32),
        compiler_params=pltpu.CompilerParams(
            dimension_semantics=("parallel", "arbitrary"), vmem_limit_bytes=VMEM_LIMIT_BYTES),
        name="neighbourhood_attention",
    )(qdt, kd, vdt, bias_t)


def _na_bias_table(rpb, kh, win, variants):
    rows_q = NA_QROWS
    nrow, ncol = 2 * NA_KH - 1, 2 * NA_KW - 1
    col = np.arange(GRID_W)
    cstart = np.clip(col - NA_KW // 2, 0, GRID_W - NA_KW)
    col_in = (col[None, :] >= cstart[:, None]) & (col[None, :] < cstart[:, None] + NA_KW)
    col_idx = np.clip(col[None, :] - col[:, None], 1 - NA_KW, NA_KW - 1) + NA_KW - 1
    col_sel = (col_idx.T[None] == np.arange(ncol)[:, None, None]).astype(np.float32)
    row_sel, valid = [], []
    for ds, offs in variants:
        rel = ds + np.arange(win)[:, None] - np.arange(rows_q)[None, :]
        offs = np.asarray(offs)[None, :]
        valid.append((rel >= offs) & (rel < offs + kh))
        row_sel.append((rel + NA_KH - 1)[..., None] == np.arange(nrow))
    row_sel = np.stack(row_sel).astype(np.float32)
    keep = np.stack(valid)[:, :, None, None, :, None] & col_in.T[None, None, :, None, None, :]
    pairs = rpb.reshape(rpb.shape[0] // 2, 2, nrow, ncol).astype(F32)
    tbl = jnp.einsum('vxar,pgrc,cyq->pvxygaq', row_sel, pairs, col_sel, precision=lax.Precision.HIGHEST)
    tbl = jnp.where(keep[None], tbl * LOG2E, NEG_BIG)
    return tbl.reshape(rpb.shape[0] // 2, len(variants), win * GRID_W, 2 * rows_q * GRID_W)


def _outmlp_kernel(*refs, tile_off, final):
    (a_ref, b_ref, c_ref, d_ref, mlat_ref, mctx_ref, g2_ref, wo_ref, w1_ref, w2_ref, gf_ref,
     o_ref) = refs[-12:]
    j = pl.program_id(1) + tile_off
    mod = jnp.where(j == 0, mctx_ref[0], mlat_ref[0])
    cat = jnp.concatenate([a_ref[0], b_ref[0], c_ref[0], d_ref[0]], axis=-1).astype(BF16)
    x1 = _read_tokens(refs[:-12], j == 0) + mod[2:3] * _dot(cat, wo_ref[...])
    ms = jnp.mean(x1 * x1, axis=-1, keepdims=True)
    h2 = ((x1 * lax.rsqrt(ms + EPS) * g2_ref[...]) * (1.0 + mod[4:5]) + mod[3:4]).astype(BF16)
    hid = w1_ref.shape[1]
    step = 1024
    acc = jnp.zeros(x1.shape, F32)
    nchunk = hid // step
    up = _dot(h2, w1_ref[:, 0:step])
    for c in range(nchunk):
        u = jnp.maximum(up, 0.0)
        if c + 1 < nchunk:
            up = _dot(h2, w1_ref[:, (c + 1) * step:(c + 2) * step])
        acc = acc + _dot((u * u).astype(BF16), w2_ref[c * step:(c + 1) * step, :])
    x2 = x1 + mod[5:6] * acc
    if final:
        ms2 = jnp.mean(x2 * x2, axis=-1, keepdims=True)
        x2 = x2 * lax.rsqrt(ms2 + EPS) * gf_ref[...]
    o_ref[0] = x2


def _outmlp(x_ctx, x_lat, a, bm, c, dd, mods_l, g2, wo, w1, w2, gf, final):
    b, t, _ = a.shape
    d = x_lat.shape[-1]
    tile_off = CTX_LEN // ROW_TILE if final else 0
    nt = t // ROW_TILE - tile_off
    ctx_row = mods_l.shape[0] - 1
    full2 = lambda bb, j: (0, 0)
    grp = pl.BlockSpec((1, ROW_TILE, GROUP_WIDTH), lambda bb, j: (bb, j + tile_off, 0))
    tokens, token_specs = _token_sources(x_ctx, x_lat, tile_off)
    return pl.pallas_call(
        functools.partial(_outmlp_kernel, tile_off=tile_off, final=final),
        grid=(b, nt),
        in_specs=token_specs + [
            grp, grp, grp, grp,
            pl.BlockSpec((1, 6, d), lambda bb, j: (bb, 0, 0)),
            pl.BlockSpec((1, 6, d), lambda bb, j: (ctx_row, 0, 0)),
            pl.BlockSpec((1, d), full2),
            pl.BlockSpec(wo.shape, full2),
            pl.BlockSpec(w1.shape, full2),
            pl.BlockSpec(w2.shape, full2),
            pl.BlockSpec((1, d), full2),
        ],
        out_specs=pl.BlockSpec((1, ROW_TILE, d), lambda bb, j: (bb, j, 0)),
        out_shape=jax.ShapeDtypeStruct((b, nt * ROW_TILE, d), F32),
        compiler_params=pltpu.CompilerParams(
            dimension_semantics=("parallel", "arbitrary"), vmem_limit_bytes=VMEM_LIMIT_BYTES),
        name="out_projection_mlp",
    )(*tokens, a, bm, c, dd, mods_l, mods_l, g2, wo, w1, w2, gf)


def _pack_in_projection(w):
    gw, kvw = GROUP_WIDTH, KV_HEADS * HEAD_DIM
    g0 = 9 * gw
    c0 = g0 + 4 * GROUP_HEADS
    d0 = c0 + gw + 2 * kvw

    def pair_swapped(cols):
        n = cols.shape[1]
        return cols.reshape(cols.shape[0], n // 2, 2)[:, :, ::-1].reshape(cols.shape[0], n)

    q, k = w[:, c0:c0 + gw], w[:, c0 + gw:c0 + gw + kvw]
    parts = [w[:, 0:g0], w[:, g0:c0], jnp.zeros((w.shape[0], G_W - 4 * GROUP_HEADS), w.dtype),
             q, pair_swapped(q), k, pair_swapped(k), w[:, c0 + gw + kvw:d0], w[:, d0:d0 + 3 * gw]]
    packed = jnp.concatenate(parts, axis=1).astype(BF16)
    assert packed.shape[1] == IN_PACKED
    return packed


def _rope_tables(n_tokens):
    t = np.arange(n_tokens)
    row = (t // GRID_W).astype(np.float32)
    col = (t % GRID_W).astype(np.float32)
    axis_dims = HEAD_DIM // 2
    inv = np.power(np.float32(ROPE_THETA),
                   (-2.0 * np.arange(axis_dims // 2, dtype=np.float32) / axis_dims).astype(np.float32))
    ang = np.concatenate([row[:, None] * inv, col[:, None] * inv], axis=-1).astype(np.float32)
    cos = np.repeat(np.cos(ang).astype(np.float32), 2, axis=-1)
    sign = np.where(np.arange(HEAD_DIM) % 2 == 0, -1.0, 1.0).astype(np.float32)
    sin = np.repeat(np.sin(ang).astype(np.float32), 2, axis=-1) * sign
    cos = np.concatenate([np.ones((CTX_LEN, HEAD_DIM), np.float32), cos], axis=0)
    sin = np.concatenate([np.zeros((CTX_LEN, HEAD_DIM), np.float32), sin], axis=0)
    return jnp.asarray(np.tile(cos, (1, GROUP_HEADS))), jnp.asarray(np.tile(sin, (1, GROUP_HEADS)))


def kernel(x, c, ctx, c_ctx, w_mod, b_mod, norm1_g, norm2_g, w_in, hgrn_lb_logits, hgrn_norm_g,
           mlstm_gate_b, mlstm_norm_g, gqa_qnorm_g, gqa_knorm_g, na_rpb, w_out, w_mlp1, w_mlp2,
           final_norm_g):
    bsz, seq, d = x.shape
    depth = w_mod.shape[0]
    rows = seq // GRID_W

    n_c = bsz + 1
    pad = (-n_c) % 8
    cc = jnp.concatenate([c, c_ctx[None, :], jnp.zeros((pad, d), F32)], axis=0)
    mods = _modulation(cc, w_mod, b_mod)[:, :n_c].reshape(depth, n_c, 6, d)

    cos4, sin4 = _rope_tables(seq)
    swap = np.arange(HEAD_DIM) ^ 1

    x_ctx, x_lat = ctx, x
    out = None
    for l in range(depth):
        need_ctx = l < depth - 1
        w_packed = _pack_in_projection(w_in[l])
        gq = jnp.tile(gqa_qnorm_g[l], GROUP_HEADS)[None, :]
        gqs = jnp.tile(gqa_qnorm_g[l][swap], GROUP_HEADS)[None, :]
        gk = jnp.tile(gqa_knorm_g[l], KV_HEADS)[None, :]
        gks = jnp.tile(gqa_knorm_g[l][swap], KV_HEADS)[None, :]
        za, zb, zg, qc, kc, vc, qd, kd, vd = _inproj(
            x_ctx, x_lat, mods[l], norm1_g[l][None, :], w_packed, cos4, sin4, gq, gqs, gk, gks)

        a = _hgrn(za, hgrn_lb_logits, jnp.tile(hgrn_norm_g[l], GROUP_HEADS)[None, :], l, need_ctx)
        gate_b = jnp.concatenate([mlstm_gate_b[l], jnp.zeros((G_W - 4 * GROUP_HEADS,), F32)])[None, :]
        bm = _mlstm(zb, zg, gate_b, jnp.tile(mlstm_norm_g[l], GROUP_HEADS)[None, :], need_ctx)
        cg = _gqa(qc, kc, vc, need_ctx)
        kh, win, blocks, variants = _na_plan(rows)
        dg = _na(qd, kd, vd, _na_bias_table(na_rpb[l], kh, win, variants), blocks, need_ctx)

        res = _outmlp(x_ctx, x_lat, a, bm, cg, dg, mods[l], norm2_g[l][None, :], w_out[l].astype(BF16),
                      w_mlp1[l].astype(BF16), w_mlp2[l].astype(BF16), final_norm_g[None, :],
                      final=not need_ctx)
        if need_ctx:
            x_ctx = x_lat = res
        else:
            out = res
    return out
```

```python
import functools

import numpy as np
import jax
import jax.numpy as jnp
from jax import lax
from jax.experimental import pallas as pl
from jax.experimental.pallas import tpu as pltpu

F32 = jnp.float32
BF16 = jnp.bfloat16

D_MODEL = 1024
GRID_W = 64
CTX_LEN = 256
HEAD_DIM = 64
GROUP_HEADS = 4
GROUP_WIDTH = GROUP_HEADS * HEAD_DIM
KV_HEADS = 2
MLP_HIDDEN = 4 * D_MODEL
CHUNK = 64
SUB = 8
NA_KH = 8
NA_KW = 16
ROPE_THETA = 10000.0
EPS = 1e-6
NEG_BIG = -1e30
FORGET_FLOOR = 1e-20
ROW_TILE = 256
VMEM_LIMIT_BYTES = 56 * 1024 * 1024
LOG2E = 1.4426950408889634
Q_SCALE = HEAD_DIM ** -0.5 * LOG2E
NA_QROWS = 4
NA_AHEAD = 2
GQA_Q_TILES = 3
GQA_AHEAD = 2

A_OFF, A_W = 0, 5 * GROUP_WIDTH
B_OFF, B_W = A_OFF + A_W, 4 * GROUP_WIDTH
G_OFF, G_W = B_OFF + B_W, 128
C_OFF, C_W = G_OFF + G_W, 896
D_OFF, D_W = C_OFF + C_W, 3 * GROUP_WIDTH
IN_PACKED = D_OFF + D_W


def _dot(a, b):
    return jnp.dot(a, b, preferred_element_type=F32)


def _dot_nt(a, b):
    return lax.dot_general(a, b, (((1,), (1,)), ((), ())), preferred_element_type=F32)


def _dot_tn(a, b):
    return lax.dot_general(a, b, (((0,), (0,)), ((), ())), preferred_element_type=F32)


def _split3(x):
    hi = x.astype(BF16)
    r = x - hi.astype(F32)
    mid = r.astype(BF16)
    lo = (r - mid.astype(F32)).astype(BF16)
    return hi, mid, lo


def _route_l(m01, x):
    hi, mid, lo = _split3(x)
    return _dot(m01, hi) + _dot(m01, mid) + _dot(m01, lo)


def _route_r2(x, m01):
    hi = x.astype(BF16)
    mid = (x - hi.astype(F32)).astype(BF16)
    return _dot(hi, m01) + _dot(mid, m01)


def _sigmoid(x):
    return 1.0 / (1.0 + jnp.exp(-x))


def _log_sigmoid(x):
    return -(jnp.maximum(-x, 0.0) + jnp.log1p(jnp.exp(-jnp.abs(x))))


def _iota(shape, dim):
    return lax.broadcasted_iota(jnp.int32, shape, dim)


def _head_block_ones(n, m, dtype):
    return jnp.where((_iota((n, m), 0) >> 6) == (_iota((n, m), 1) >> 6), 1.0, 0.0).astype(dtype)


def _head_rms(o, gain, ones_bd):
    ssq = _route_r2(o * o, ones_bd)
    return o * lax.rsqrt(ssq * (1.0 / HEAD_DIM) + EPS) * gain


def _mod_kernel(c_ref, w_ref, b_ref, o_ref):
    c = c_ref[...]
    s = c * _sigmoid(c)
    o_ref[0] = _dot(s.astype(BF16), w_ref[0].astype(BF16)) + b_ref[0]


def _modulation(cc, w_mod, b_mod):
    depth, d, n = w_mod.shape
    rows = cc.shape[0]
    tn = 1024
    return pl.pallas_call(
        _mod_kernel,
        grid=(depth, n // tn),
        in_specs=[
            pl.BlockSpec((rows, d), lambda l, j: (0, 0)),
            pl.BlockSpec((1, d, tn), lambda l, j: (l, 0, j)),
            pl.BlockSpec((1, 1, tn), lambda l, j: (l, 0, j)),
        ],
        out_specs=pl.BlockSpec((1, rows, tn), lambda l, j: (l, 0, j)),
        out_shape=jax.ShapeDtypeStruct((depth, rows, n), F32),
        compiler_params=pltpu.CompilerParams(
            dimension_semantics=("arbitrary", "arbitrary"), vmem_limit_bytes=VMEM_LIMIT_BYTES),
        name="adaln_modulation",
    )(cc, w_mod, b_mod.reshape(depth, 1, n))


def _read_tokens(refs, is_ctx_tile):
    if len(refs) == 1:
        return refs[0][0]
    return jnp.where(is_ctx_tile, refs[0][0], refs[1][0])


def _inproj_kernel(*refs):
    (mlat_ref, mctx_ref, g1_ref, w_ref, cos_ref, sin_ref, gq_ref, gqs_ref, gk_ref, gks_ref,
     za_ref, zb_ref, zg_ref, qc_ref, kc_ref, vc_ref, qd_ref, kd_ref, vd_ref) = refs[-19:]
    j = pl.program_id(1)
    mod = jnp.where(j == 0, mctx_ref[0], mlat_ref[0])
    x = _read_tokens(refs[:-19], j == 0)
    ms = jnp.mean(x * x, axis=-1, keepdims=True)
    y = x * lax.rsqrt(ms + EPS) * g1_ref[...]
    h = (y * (1.0 + mod[1:2]) + mod[0:1]).astype(BF16)

    zc = _dot(h, w_ref[:, C_OFF:C_OFF + C_W])
    zd = _dot(h, w_ref[:, D_OFF:D_OFF + D_W])
    za_ref[0] = _dot(h, w_ref[:, A_OFF:A_OFF + A_W])

    ones4 = _head_block_ones(GROUP_WIDTH, GROUP_WIDTH, BF16)
    cos, sin = cos_ref[...], sin_ref[...]
    q, qs = zc[:, 0:256], zc[:, 256:512]
    k, ks = zc[:, 512:640], zc[:, 640:768]
    ssq_q = _route_r2(q * q, ones4)
    ssq_k = _route_r2(k * k, ones4[0:128, 0:128])
    zb_ref[0] = _dot(h, w_ref[:, B_OFF:B_OFF + B_W])
    zg_ref[0] = _dot(h, w_ref[:, G_OFF:G_OFF + G_W])
    rq = lax.rsqrt(ssq_q * (1.0 / HEAD_DIM) + EPS)
    qrt = ((rq * (q * gq_ref[...] * cos + qs * gqs_ref[...] * sin)) * Q_SCALE).T
    rk = lax.rsqrt(ssq_k * (1.0 / HEAD_DIM) + EPS)
    kr = (rk * (k * gk_ref[...] * cos[:, 0:128] + ks * gks_ref[...] * sin[:, 0:128])).astype(BF16)
    vvt = zc[:, 768:896].T
    for hh in range(GROUP_HEADS):
        qc_ref[0, hh] = qrt[hh * HEAD_DIM:(hh + 1) * HEAD_DIM].astype(BF16)
    for hh in range(KV_HEADS):
        kc_ref[0, hh] = kr[:, hh * HEAD_DIM:(hh + 1) * HEAD_DIM]
        vc_ref[0, hh] = vvt[hh * HEAD_DIM:(hh + 1) * HEAD_DIM].astype(BF16)

    qdt = (zd[:, 0:256] * Q_SCALE).T
    kdb = zd[:, 256:512].astype(BF16)
    vdt = zd[:, 512:768].T
    for pp in range(GROUP_HEADS // 2):
        lo, hi = pp * 2 * HEAD_DIM, (pp + 1) * 2 * HEAD_DIM
        qd_ref[0, pp] = qdt[lo:hi].astype(BF16)
        kd_ref[0, pp] = kdb[:, lo:hi]
        vd_ref[0, pp] = vdt[lo:hi].astype(BF16)


def _token_sources(x_ctx, x_lat, tile_off=0):
    assert CTX_LEN == ROW_TILE
    d = x_lat.shape[-1]
    if x_lat is x_ctx:
        return [x_lat], [pl.BlockSpec((1, ROW_TILE, d), lambda bb, j: (bb, j + tile_off, 0))]
    return [x_ctx, x_lat], [pl.BlockSpec((1, ROW_TILE, d), lambda bb, j: (bb, 0, 0)),
                            pl.BlockSpec((1, ROW_TILE, d), lambda bb, j: (bb, jnp.maximum(j + tile_off, 1) - 1, 0))]


def _inproj(x_ctx, x_lat, mods_l, g1, w_packed, cos4, sin4, gq, gqs, gk, gks):
    b, _, d = x_lat.shape
    t = x_lat.shape[1] if x_lat is x_ctx else x_ctx.shape[1] + x_lat.shape[1]
    nt = t // ROW_TILE
    ctx_row = mods_l.shape[0] - 1
    full2 = lambda bb, j: (0, 0)
    headed = lambda nh, w=HEAD_DIM: pl.BlockSpec((1, nh, ROW_TILE, w), lambda bb, j: (bb, 0, j, 0))
    headed_t = lambda nh, w=HEAD_DIM: pl.BlockSpec((1, nh, w, ROW_TILE), lambda bb, j: (bb, 0, 0, j))
    flat = lambda w: pl.BlockSpec((1, ROW_TILE, w), lambda bb, j: (bb, j, 0))
    hshape = lambda nh, w=HEAD_DIM: jax.ShapeDtypeStruct((b, nh, t, w), BF16)
    hshape_t = lambda nh, w=HEAD_DIM: jax.ShapeDtypeStruct((b, nh, w, t), BF16)
    fshape = lambda w: jax.ShapeDtypeStruct((b, t, w), F32)
    pw = 2 * HEAD_DIM
    tokens, token_specs = _token_sources(x_ctx, x_lat)
    return pl.pallas_call(
        _inproj_kernel,
        grid=(b, nt),
        in_specs=token_specs + [
            pl.BlockSpec((1, 6, d), lambda bb, j: (bb, 0, 0)),
            pl.BlockSpec((1, 6, d), lambda bb, j: (ctx_row, 0, 0)),
            pl.BlockSpec((1, d), full2),
            pl.BlockSpec((d, IN_PACKED), full2),
            pl.BlockSpec((ROW_TILE, GROUP_WIDTH), lambda bb, j: (j, 0)),
            pl.BlockSpec((ROW_TILE, GROUP_WIDTH), lambda bb, j: (j, 0)),
            pl.BlockSpec((1, GROUP_WIDTH), full2),
            pl.BlockSpec((1, GROUP_WIDTH), full2),
            pl.BlockSpec((1, 128), full2),
            pl.BlockSpec((1, 128), full2),
        ],
        out_specs=[flat(A_W), flat(B_W), flat(G_W), headed_t(4), headed(2), headed_t(2),
                   headed_t(2, pw), headed(2, pw), headed_t(2, pw)],
        out_shape=[fshape(A_W), fshape(B_W), fshape(G_W), hshape_t(4), hshape(2), hshape_t(2),
                   hshape_t(2, pw), hshape(2, pw), hshape_t(2, pw)],
        compiler_params=pltpu.CompilerParams(
            dimension_semantics=("parallel", "arbitrary"), vmem_limit_bytes=VMEM_LIMIT_BYTES),
        name="in_projection",
    )(*tokens, mods_l, mods_l, g1, w_packed, cos4, sin4, gq, gqs, gk, gks)


def _hgrn_keys_decay(za_ref, row0, lb, rev):
    fcol = 4 * GROUP_WIDTH if rev else 3 * GROUP_WIDTH
    zf = za_ref[0, pl.ds(row0, ROW_TILE), fcol:fcol + GROUP_WIDTH]
    sg = _sigmoid(zf)
    g = jnp.log(jnp.maximum(lb + (1.0 - lb) * sg, FORGET_FLOOR))
    return (1.0 - lb) * (1.0 - sg), g


def _hgrn_tile_gates(za_ref, row0, lb, rev, cmat4):
    k, g = _hgrn_keys_decay(za_ref, row0, lb, rev)
    return k, _route_l(cmat4, g)


HGRN_SCALES = (1, 2, 4, 8, 16, 32)


def _hgrn_scale_consts(rev):
    t = _iota((CHUNK, GROUP_WIDTH), 0)
    s = _iota((CHUNK, GROUP_WIDTH), 1) & (HEAD_DIM - 1)
    out = []
    for c in HGRN_SCALES:
        sh = c.bit_length()
        t_hi = (t & (2 * c - 1)) >= c
        s_hi = (s & (2 * c - 1)) >= c
        isq = jnp.logical_not(t_hi) if rev else t_hi
        iskey = s_hi if rev else jnp.logical_not(s_hi)
        pair = jnp.where(((t >> sh) == (s >> sh)) & isq & iskey, 1.0, 0.0).astype(F32)
        out.append((isq, jnp.where(isq, 1.0, -1.0).astype(F32), pair))
    return out, jnp.where(t == s, 1.0, 0.0).astype(F32)


def _hgrn_boundary_rows(b, c, rev):
    off = c if rev else c - 1
    if 2 * c >= SUB:
        pieces = [jnp.broadcast_to(b[lo + off:lo + off + 1], (2 * c, b.shape[1]))
                  for lo in range(0, CHUNK, 2 * c)]
        return pieces[0] if len(pieces) == 1 else jnp.concatenate(pieces, axis=0)
    pos = _iota(b.shape, 0) & (2 * c - 1)
    out = b
    for j in range(2 * c):
        if j != off:
            out = jnp.where(pos == j, pltpu.roll(b, (j - off) % CHUNK, 0), out)
    return out


def _hgrn_chunk_local(za_ref, row0, k, b, rev, want_out, consts):
    ones_bd, bmask, scales, diag = consts
    v = za_ref[0, pl.ds(row0, CHUNK), GROUP_WIDTH:2 * GROUP_WIDTH]
    b_end = b[0:1] if rev else b[CHUNK - 1:CHUNK]
    vb = v.astype(BF16)
    upd = _dot_tn(vb, (k * jnp.exp(b_end - b)).astype(BF16)) * bmask
    if not want_out:
        return jnp.exp(b_end), upd, None, None
    zq = za_ref[0, pl.ds(row0, CHUNK), 0:GROUP_WIDTH]
    q = zq * _sigmoid(zq) * (HEAD_DIM ** -0.5)
    stack = lambda x: jnp.concatenate([x] * GROUP_HEADS, axis=0) * ones_bd
    a = _dot_nt(q.astype(BF16), stack(k.astype(BF16))) * diag
    for c, (isq, sign, pair) in zip(HGRN_SCALES, scales):
        d = b - _hgrn_boundary_rows(b, c, rev)
        x = (jnp.where(isq, q, k) * jnp.exp(d * sign)).astype(BF16)
        a = a + _dot_nt(x, stack(x)) * pair
    return jnp.exp(b_end), upd, (q * jnp.exp(b)).astype(BF16), _dot(a.astype(BF16), stack(vb))


def _hgrn_chunk_carry(o_ref, row0, st, local):
    decay, upd, qe, o_intra = local
    if qe is not None:
        o_ref[0, pl.ds(row0, CHUNK), :] += o_intra + _dot_nt(qe, st.astype(BF16))
    return st * decay + upd


def _hgrn_kernel(za_ref, lg_ref, gain_ref, o_ref, stf_ref, stb_ref, *, layer, need_ctx):
    t = za_ref.shape[1]
    n_tiles = t // ROW_TILE
    per_tile = ROW_TILE // CHUNK
    lg = lg_ref[...]
    depth = lg.shape[0]
    mx = lg[0]
    for i in range(1, depth):
        mx = jnp.maximum(mx, lg[i])
    es = [jnp.exp(lg[i] - mx) for i in range(depth)]
    tot = es[0]
    for i in range(1, depth):
        tot = tot + es[i]
    sm = [e / tot for e in es]
    cs = sm[0]
    for i in range(1, layer + 1):
        cs = cs + sm[i]
    lbs = cs - sm[0]
    lb_f, lb_b = lbs[0:1], lbs[1:2]

    ones_bd = _head_block_ones(GROUP_WIDTH, GROUP_WIDTH, BF16)
    bmask = _head_block_ones(GROUP_WIDTH, GROUP_WIDTH, F32)
    consts_f = (ones_bd, bmask) + _hgrn_scale_consts(False)
    consts_b = (ones_bd, bmask) + _hgrn_scale_consts(True)
    r4, c4 = _iota((ROW_TILE, ROW_TILE), 0), _iota((ROW_TILE, ROW_TILE), 1)
    same_chunk = (r4 >> 6) == (c4 >> 6)
    cum_f = jnp.where(same_chunk & (c4 <= r4), 1.0, 0.0).astype(BF16)
    cum_b = jnp.where(same_chunk & (c4 >= r4), 1.0, 0.0).astype(BF16)

    stf_ref[...] = jnp.zeros_like(stf_ref)
    stb_ref[...] = jnp.zeros_like(stb_ref)
    o_ref[...] = jnp.zeros_like(o_ref)

    def tile_step(row_f, row_b, want):
        k_f, b_f = _hgrn_tile_gates(za_ref, row_f, lb_f, False, cum_f)
        k_b, b_b = _hgrn_tile_gates(za_ref, row_b, lb_b, True, cum_b)
        lo_f = [c * CHUNK for c in range(per_tile)]
        lo_b = [(per_tile - 1 - c) * CHUNK for c in range(per_tile)]
        loc_f = [_hgrn_chunk_local(za_ref, row_f + lo, k_f[lo:lo + CHUNK], b_f[lo:lo + CHUNK], False, want,
                                   consts_f) for lo in lo_f]
        loc_b = [_hgrn_chunk_local(za_ref, row_b + lo, k_b[lo:lo + CHUNK], b_b[lo:lo + CHUNK], True, want,
                                   consts_b) for lo in lo_b]
        st_f, st_b = stf_ref[...], stb_ref[...]
        for c in range(per_tile):
            st_f = _hgrn_chunk_carry(o_ref, row_f + lo_f[c], st_f, loc_f[c])
            st_b = _hgrn_chunk_carry(o_ref, row_b + lo_b[c], st_b, loc_b[c])
        stf_ref[...] = st_f
        stb_ref[...] = st_b

    n_ctx_tiles = CTX_LEN // ROW_TILE
    for i in range(n_ctx_tiles):
        tile_step(i * ROW_TILE, (n_ctx_tiles - 1 - i) * ROW_TILE, need_ctx)

    def lat_body(i, carry):
        tile_step(pl.multiple_of((n_ctx_tiles + i) * ROW_TILE, ROW_TILE),
                  pl.multiple_of((n_tiles - 1 - i) * ROW_TILE, ROW_TILE), True)
        return carry

    lax.fori_loop(0, n_tiles - n_ctx_tiles, lat_body, 0)

    gain = gain_ref[...]

    def fin_body(i, carry):
        r0 = pl.multiple_of(i * ROW_TILE, ROW_TILE)
        o = o_ref[0, pl.ds(r0, ROW_TILE), :]
        zo = za_ref[0, pl.ds(r0, ROW_TILE), 2 * GROUP_WIDTH:3 * GROUP_WIDTH]
        o_ref[0, pl.ds(r0, ROW_TILE), :] = _head_rms(o, gain, ones_bd) * (zo * _sigmoid(zo))
        return carry

    lax.fori_loop(0, t // ROW_TILE, fin_body, 0)


def _hgrn(za, lb_logits, gain4, layer, need_ctx):
    b, t, _ = za.shape
    depth = lb_logits.shape[0]
    return pl.pallas_call(
        functools.partial(_hgrn_kernel, layer=layer, need_ctx=need_ctx),
        grid=(b,),
        in_specs=[
            pl.BlockSpec((1, t, A_W), lambda bb: (bb, 0, 0)),
            pl.BlockSpec((depth, 2, GROUP_WIDTH), lambda bb: (0, 0, 0)),
            pl.BlockSpec((1, GROUP_WIDTH), lambda bb: (0, 0)),
        ],
        out_specs=pl.BlockSpec((1, t, GROUP_WIDTH), lambda bb: (bb, 0, 0)),
        out_shape=jax.ShapeDtypeStruct((b, t, GROUP_WIDTH), F32),
        scratch_shapes=[pltpu.VMEM((GROUP_WIDTH, GROUP_WIDTH), F32),
                        pltpu.VMEM((GROUP_WIDTH, GROUP_WIDTH), F32)],
        compiler_params=pltpu.CompilerParams(
            dimension_semantics=("parallel",), vmem_limit_bytes=VMEM_LIMIT_BYTES),
        name="hgrn2_scan",
    )(za, lb_logits, gain4)


def _mlstm_gate_routes(zg_ref, row0, gate_b, consts):
    e_i, e_f = consts[1:3]
    hi, mid, lo = _split3(zg_ref[0, pl.ds(row0, ROW_TILE), :] + gate_b)
    ig = _dot(hi, e_i) + _dot(mid, e_i) + _dot(lo, e_i)
    return ig, _log_sigmoid(_dot(hi, e_f) + _dot(mid, e_f) + _dot(lo, e_f))


def _mlstm_tile_gates(zg_ref, row0, gate_b, consts):
    ig, lf = _mlstm_gate_routes(zg_ref, row0, gate_b, consts)
    return ig, _route_l(consts[0], lf)


def _mlstm_chunk_local(zb_ref, row0, ig, bc, rev, want_out, consts):
    ones_bd, bmask, diag, causal = consts[3:7]
    k = zb_ref[0, pl.ds(row0, CHUNK), GROUP_WIDTH:2 * GROUP_WIDTH] * (HEAD_DIM ** -0.5)
    v = zb_ref[0, pl.ds(row0, CHUNK), 2 * GROUP_WIDTH:3 * GROUP_WIDTH]
    b_end = bc[0:1] if rev else bc[CHUNK - 1:CHUNK]
    vb = v.astype(BF16)
    gs = b_end - bc + ig
    g_max = jnp.max(gs, axis=0, keepdims=True)
    kw = k * jnp.exp(gs - g_max)
    loc = dict(b_end=b_end, g_max=g_max, bc=bc, upd=_dot_tn(vb, kw.astype(BF16)) * bmask,
               nupd=jnp.sum(kw, axis=0, keepdims=True), q=None)
    if want_out:
        q = zb_ref[0, pl.ds(row0, CHUNK), 0:GROUP_WIDTH]
        qb = q.astype(BF16)
        rowv = jnp.sum((ig - bc) * diag, axis=0, keepdims=True)
        dlog = jnp.where(causal, bc + rowv, NEG_BIG)
        segs = []
        for hh in range(GROUP_HEADS):
            mh = jnp.max(dlog[:, hh * HEAD_DIM:(hh + 1) * HEAD_DIM], axis=-1, keepdims=True)
            segs.append(jnp.broadcast_to(mh, (CHUNK, HEAD_DIM)))
        m_loc = jnp.concatenate(segs, axis=-1)
        kk = jnp.concatenate([k.astype(BF16)] * GROUP_HEADS, axis=0) * ones_bd
        vv = jnp.concatenate([vb] * GROUP_HEADS, axis=0) * ones_bd
        pb = (jnp.exp(dlog - m_loc) * _dot_nt(qb, kk)).astype(BF16)
        loc.update(q=q, qb=qb, m_loc=m_loc, num_loc=_dot(pb, vv), den_loc=_dot(pb, ones_bd))
    return loc


def _mlstm_chunk_carry(o_ref, row0, state, loc, ones_bd):
    ct, n_prev, m_prev = state
    if loc["q"] is not None:
        qc = _dot_nt(loc["qb"], ct.astype(BF16))
        qn = _dot((loc["q"] * n_prev).astype(BF16), ones_bd)
        a = loc["bc"] + m_prev
        m_t = jnp.maximum(a, loc["m_loc"])
        w_in = jnp.exp(a - m_t)
        w_loc = jnp.exp(loc["m_loc"] - m_t)
        num = w_in * qc + w_loc * loc["num_loc"]
        den = w_in * qn + w_loc * loc["den_loc"]
        o_ref[0, pl.ds(row0, CHUNK), :] += num / jnp.maximum(jnp.abs(den), jnp.exp(-m_t))
    m_new = jnp.maximum(loc["b_end"] + m_prev, loc["g_max"])
    w_old = jnp.exp(loc["b_end"] + m_prev - m_new)
    w_upd = jnp.exp(loc["g_max"] - m_new)
    return ct * w_old + loc["upd"] * w_upd, n_prev * w_old + loc["nupd"] * w_upd, m_new


def _mlstm_kernel(zb_ref, zg_ref, gb_ref, gain_ref, o_ref, ctf_ref, ctb_ref, nf_ref, nb_ref,
                  mf_ref, mb_ref, gates_ref, *, need_ctx):
    t = zb_ref.shape[1]
    n_tiles = t // ROW_TILE
    per_tile = ROW_TILE // CHUNK
    ones_bd = _head_block_ones(GROUP_WIDTH, GROUP_WIDTH, BF16)
    bmask = _head_block_ones(GROUP_WIDTH, GROUP_WIDTH, F32)
    gcol = _iota((128, GROUP_WIDTH), 0)
    ghead = _iota((128, GROUP_WIDTH), 1) >> 6
    route = lambda base: jnp.where(gcol == base + ghead, 1.0, 0.0).astype(BF16)
    tt = _iota((CHUNK, GROUP_WIDTH), 0)
    ss = _iota((CHUNK, GROUP_WIDTH), 1) & (HEAD_DIM - 1)
    diag = jnp.where(tt == ss, 1.0, 0.0).astype(F32)
    r4, c4 = _iota((ROW_TILE, ROW_TILE), 0), _iota((ROW_TILE, ROW_TILE), 1)
    same_chunk = (r4 >> 6) == (c4 >> 6)
    cum4 = lambda rev: jnp.where(same_chunk & ((c4 >= r4) if rev else (c4 <= r4)), 1.0, 0.0).astype(BF16)
    cf = (cum4(False), route(0), route(2 * GROUP_HEADS), ones_bd, bmask, diag, ss <= tt)
    cb = (cum4(True), route(GROUP_HEADS), route(3 * GROUP_HEADS), ones_bd, bmask, diag, ss >= tt)
    gate_b = gb_ref[...]

    for r in (ctf_ref, ctb_ref, nf_ref, nb_ref, mf_ref, mb_ref, o_ref):
        r[...] = jnp.zeros_like(r)

    def tile_gates(row_f, row_b):
        return (_mlstm_tile_gates(zg_ref, row_f, gate_b, cf) + _mlstm_tile_gates(zg_ref, row_b, gate_b, cb))

    def tile_step(row_f, row_b, want, gates, next_rows=None):
        ig_f, bc_f, ig_b, bc_b = gates
        lo_f = [c * CHUNK for c in range(per_tile)]
        lo_b = [(per_tile - 1 - c) * CHUNK for c in range(per_tile)]
        st_f = (ctf_ref[...], nf_ref[0:1], mf_ref[0:1])
        st_b = (ctb_ref[...], nb_ref[0:1], mb_ref[0:1])
        nxt = None
        for c in range(per_tile):
            if next_rows is not None and c == 1:
                nxt = [_mlstm_gate_routes(zg_ref, r, gate_b, cc) for r, cc in zip(next_rows, (cf, cb))]
            if next_rows is not None and c == 3:
                nxt = (nxt[0][0], _route_l(cf[0], nxt[0][1]), nxt[1][0], _route_l(cb[0], nxt[1][1]))
            lo = lo_f[c]
            loc = _mlstm_chunk_local(zb_ref, row_f + lo, ig_f[lo:lo + CHUNK], bc_f[lo:lo + CHUNK], False, want, cf)
            st_f = _mlstm_chunk_carry(o_ref, row_f + lo, st_f, loc, ones_bd)
            lo = lo_b[c]
            loc = _mlstm_chunk_local(zb_ref, row_b + lo, ig_b[lo:lo + CHUNK], bc_b[lo:lo + CHUNK], True, want, cb)
            st_b = _mlstm_chunk_carry(o_ref, row_b + lo, st_b, loc, ones_bd)
        for refs, st in (((ctf_ref, nf_ref, mf_ref), st_f), ((ctb_ref, nb_ref, mb_ref), st_b)):
            refs[0][...] = st[0]
            refs[1][...] = jnp.broadcast_to(st[1], refs[1].shape)
            refs[2][...] = jnp.broadcast_to(st[2], refs[2].shape)
        return nxt

    n_ctx_tiles = CTX_LEN // ROW_TILE
    n_lat = n_tiles - n_ctx_tiles
    lat_rows = lambda i: (pl.multiple_of(jnp.minimum(n_ctx_tiles + i, n_tiles - 1) * ROW_TILE, ROW_TILE),
                          pl.multiple_of(jnp.maximum(n_tiles - 1 - i, 0) * ROW_TILE, ROW_TILE))

    def park(gates):
        for n, g in enumerate(gates):
            gates_ref[n] = g

    gates = tile_gates(0, (n_ctx_tiles - 1) * ROW_TILE)
    for i in range(n_ctx_tiles):
        if i + 1 < n_ctx_tiles:
            next_rows = ((i + 1) * ROW_TILE, (n_ctx_tiles - 2 - i) * ROW_TILE)
        else:
            next_rows = lat_rows(0)
        gates = tile_step(i * ROW_TILE, (n_ctx_tiles - 1 - i) * ROW_TILE, need_ctx, gates, next_rows)
    park(gates)

    def lat_body(i, carry):
        row_f, row_b = lat_rows(i)
        park(tile_step(row_f, row_b, True, tuple(gates_ref[n] for n in range(4)), lat_rows(i + 1)))
        return carry

    lax.fori_loop(0, n_lat, lat_body, 0)

    gain = gain_ref[...]

    def fin_body(i, carry):
        r0 = pl.multiple_of(i * ROW_TILE, ROW_TILE)
        o = o_ref[0, pl.ds(r0, ROW_TILE), :]
        zo = zb_ref[0, pl.ds(r0, ROW_TILE), 3 * GROUP_WIDTH:4 * GROUP_WIDTH]
        o_ref[0, pl.ds(r0, ROW_TILE), :] = _head_rms(o, gain, ones_bd) * _sigmoid(zo)
        return carry

    lax.fori_loop(0, t // ROW_TILE, fin_body, 0)


def _mlstm(zb, zg, gate_b128, gain4, need_ctx):
    b, t, _ = zb.shape
    return pl.pallas_call(
        functools.partial(_mlstm_kernel, need_ctx=need_ctx),
        grid=(b,),
        in_specs=[
            pl.BlockSpec((1, t, B_W), lambda bb: (bb, 0, 0)),
            pl.BlockSpec((1, t, G_W), lambda bb: (bb, 0, 0)),
            pl.BlockSpec((1, G_W), lambda bb: (0, 0)),
            pl.BlockSpec((1, GROUP_WIDTH), lambda bb: (0, 0)),
        ],
        out_specs=pl.BlockSpec((1, t, GROUP_WIDTH), lambda bb: (bb, 0, 0)),
        out_shape=jax.ShapeDtypeStruct((b, t, GROUP_WIDTH), F32),
        scratch_shapes=[pltpu.VMEM((GROUP_WIDTH, GROUP_WIDTH), F32),
                        pltpu.VMEM((GROUP_WIDTH, GROUP_WIDTH), F32),
                        pltpu.VMEM((SUB, GROUP_WIDTH), F32), pltpu.VMEM((SUB, GROUP_WIDTH), F32),
                        pltpu.VMEM((SUB, GROUP_WIDTH), F32), pltpu.VMEM((SUB, GROUP_WIDTH), F32),
                        pltpu.VMEM((4, ROW_TILE, GROUP_WIDTH), F32)],
        compiler_params=pltpu.CompilerParams(
            dimension_semantics=("parallel",), vmem_limit_bytes=VMEM_LIMIT_BYTES),
        name="mlstm_scan",
    )(zb, zg, gate_b128, gain4)


def _attend_t(qt, pieces):
    return _softmax_pv_t(_scores_t(qt, pieces), pieces, True)


def _scores_t(qt, pieces):
    s_list = []
    for k, _, bias in pieces:
        s = _dot(k, qt)
        s_list.append(s if bias is None else s + bias)
    return s_list


def _softmax_pv_t(s_list, pieces, mxu_denominator):
    m = jnp.max(s_list[0], axis=0, keepdims=True)
    for s in s_list[1:]:
        m = jnp.maximum(m, jnp.max(s, axis=0, keepdims=True))
    if mxu_denominator:
        pad = 2 * SUB
        acc = None
        for s, (_, vt, _) in zip(s_list, pieces):
            vt1 = jnp.concatenate([vt, jnp.ones((pad, vt.shape[1]), vt.dtype)], axis=0)
            pv = _dot(vt1, jnp.exp2(s - m).astype(BF16))
            acc = pv if acc is None else acc + pv
        hd = acc.shape[0] - pad
        return acc[0:hd] / acc[hd:hd + 1]
    l = acc = None
    for s, (_, vt, _) in zip(s_list, pieces):
        p = jnp.exp2(s - m)
        ls = jnp.sum(p, axis=0, keepdims=True)
        pv = _dot(vt, p.astype(BF16))
        l = ls if l is None else l + ls
        acc = pv if acc is None else acc + pv
    return acc / l


def _gqa_kernel(q_ref, k_ref, vt_ref, o_ref, *, need_ctx):
    j = pl.program_id(2)
    t = k_ref.shape[2]
    g = q_ref.shape[1]
    nsub = q_ref.shape[3] // ROW_TILE

    def queries(sub):
        return jnp.concatenate([q_ref[0, i, :, sub * ROW_TILE:(sub + 1) * ROW_TILE] for i in range(g)], axis=1)

    def run(has_ctx_tile):
        units = []
        for sub in range(nsub):
            if has_ctx_tile and sub < CTX_LEN // ROW_TILE:
                if need_ctx:
                    units.append((sub, [(k_ref[0, 0, 0:CTX_LEN, :], vt_ref[0, 0, :, 0:CTX_LEN], None)]))
                else:
                    o_ref[0, sub * ROW_TILE:(sub + 1) * ROW_TILE, :] = jnp.zeros((ROW_TILE, g * HEAD_DIM), F32)
            else:
                units.append((sub, [(k_ref[0, 0], vt_ref[0, 0], None)]))
        scores = {n: _scores_t(queries(units[n][0]), units[n][1]) for n in range(min(GQA_AHEAD, len(units)))}
        for n, (sub, pieces) in enumerate(units):
            if n + GQA_AHEAD < len(units):
                scores[n + GQA_AHEAD] = _scores_t(queries(units[n + GQA_AHEAD][0]), units[n + GQA_AHEAD][1])
            ot = _softmax_pv_t(scores.pop(n), pieces, False)
            o_ref[0, sub * ROW_TILE:(sub + 1) * ROW_TILE, :] = jnp.concatenate(
                [ot[:, i * ROW_TILE:(i + 1) * ROW_TILE] for i in range(g)], axis=0).T

    if nsub * ROW_TILE == t:
        run(True)
    else:
        @pl.when(j == 0)
        def _():
            run(True)

        @pl.when(j > 0)
        def _():
            run(False)


def _gqa(qct, kc, vct, need_ctx):
    b, _, t, _ = kc.shape
    g = GROUP_HEADS // KV_HEADS
    qt = GQA_Q_TILES * ROW_TILE
    assert t % qt == 0
    return pl.pallas_call(
        functools.partial(_gqa_kernel, need_ctx=need_ctx),
        grid=(b, KV_HEADS, t // qt),
        in_specs=[
            pl.BlockSpec((1, g, HEAD_DIM, qt), lambda bb, h, j: (bb, h, 0, j)),
            pl.BlockSpec((1, 1, t, HEAD_DIM), lambda bb, h, j: (bb, h, 0, 0)),
            pl.BlockSpec((1, 1, HEAD_DIM, t), lambda bb, h, j: (bb, h, 0, 0)),
        ],
        out_specs=pl.BlockSpec((1, qt, g * HEAD_DIM), lambda bb, h, j: (bb, j, h)),
        out_shape=jax.ShapeDtypeStruct((b, t, GROUP_WIDTH), F32),
        compiler_params=pltpu.CompilerParams(
            dimension_semantics=("parallel", "parallel", "arbitrary"), vmem_limit_bytes=VMEM_LIMIT_BYTES),
        name="gqa_attention",
    )(qct, kc, vct)


def _na_plan(rows):
    kh = min(NA_KH, rows)
    win = min(kh + NA_QROWS, rows)
    assert rows % NA_QROWS == 0 and (win * GRID_W) % 128 == 0
    blocks, variants = [], []
    for i in range(rows // NA_QROWS):
        qrows = np.arange(NA_QROWS * i, NA_QROWS * (i + 1))
        r0 = np.clip(qrows - kh // 2, 0, rows - kh)
        start = int(np.clip(NA_QROWS * i - kh // 2, 0, rows - win))
        assert start % 2 == 0 and start <= r0.min() and r0.max() + kh <= start + win
        key = (start - NA_QROWS * i, tuple(int(v) for v in r0 - qrows))
        if key not in variants:
            variants.append(key)
        blocks.append((start, variants.index(key)))
    return kh, win, tuple(blocks), variants


def _na_kernel(q_ref, k_ref, vt_ref, bias_ref, o_ref, *, need_ctx, blocks):
    nq = NA_QROWS * GRID_W
    nk = bias_ref.shape[2]
    kctx = k_ref[0, 0, 0:CTX_LEN, :]
    vtctx = vt_ref[0, 0, :, 0:CTX_LEN]

    def paired_queries(lo, n):
        q = q_ref[0, 0, :, lo:lo + n]
        z = jnp.zeros((HEAD_DIM, n), q.dtype)
        return jnp.concatenate([jnp.concatenate([q[0:HEAD_DIM], z], axis=1),
                                jnp.concatenate([z, q[HEAD_DIM:2 * HEAD_DIM]], axis=1)], axis=0)

    def unpair(ot, n):
        return jnp.concatenate([ot[0:HEAD_DIM, 0:n], ot[HEAD_DIM:2 * HEAD_DIM, n:2 * n]], axis=0).T

    if need_ctx:
        ot = _attend_t(paired_queries(0, CTX_LEN), [(kctx, vtctx, None)])
        o_ref[0, 0:CTX_LEN, :] = unpair(ot, CTX_LEN)
    else:
        o_ref[0, 0:CTX_LEN, :] = jnp.zeros((CTX_LEN, 2 * HEAD_DIM), F32)

    def block_pieces(i):
        start, var = blocks[i]
        k0 = CTX_LEN + start * GRID_W
        return [(kctx, vtctx, None),
                (k_ref[0, 0, k0:k0 + nk, :], vt_ref[0, 0, :, k0:k0 + nk], bias_ref[0, var])]

    scores = {i: _scores_t(paired_queries(CTX_LEN + i * nq, nq), block_pieces(i))
              for i in range(min(NA_AHEAD, len(blocks)))}
    for i in range(len(blocks)):
        q0 = CTX_LEN + i * nq
        if i + NA_AHEAD < len(blocks):
            scores[i + NA_AHEAD] = _scores_t(paired_queries(q0 + NA_AHEAD * nq, nq), block_pieces(i + NA_AHEAD))
        o_ref[0, q0:q0 + nq, :] = unpair(_softmax_pv_t(scores.pop(i), block_pieces(i), True), nq)


def _na(qdt, kd, vdt, bias_t, blocks, need_ctx):
    b, npair, t, pw = kd.shape
    nvar, nk, nq2 = bias_t.shape[1:]
    blk = pl.BlockSpec((1, 1, t, pw), lambda h, bb: (bb, h, 0, 0))
    blk_t = pl.BlockSpec((1, 1, pw, t), lambda h, bb: (bb, h, 0, 0))
    return pl.pallas_call(
        functools.partial(_na_kernel, need_ctx=need_ctx, blocks=blocks),
        grid=(npair, b),
        in_specs=[blk_t, blk, blk_t,
                  pl.BlockSpec((1, nvar, nk, nq2), lambda h, bb: (h, 0, 0, 0))],
        out_specs=pl.BlockSpec((1, t, pw), lambda h, bb: (bb, 0, h)),
        out_shape=jax.ShapeDtypeStruct((b, t, GROUP_WIDTH), F32),
        compiler_params=pltpu.CompilerParams(
            dimension_semantics=("parallel", "arbitrary"), vmem_limit_bytes=VMEM_LIMIT_BYTES),
        name="neighbourhood_attention",
    )(qdt, kd, vdt, bias_t)


def _na_bias_table(rpb, kh, win, variants):
    rows_q = NA_QROWS
    nrow, ncol = 2 * NA_KH - 1, 2 * NA_KW - 1
    col = np.arange(GRID_W)
    cstart = np.clip(col - NA_KW // 2, 0, GRID_W - NA_KW)
    col_in = (col[None, :] >= cstart[:, None]) & (col[None, :] < cstart[:, None] + NA_KW)
    col_idx = np.clip(col[None, :] - col[:, None], 1 - NA_KW, NA_KW - 1) + NA_KW - 1
    col_sel = (col_idx.T[None] == np.arange(ncol)[:, None, None]).astype(np.float32)
    row_sel, valid = [], []
    for ds, offs in variants:
        rel = ds + np.arange(win)[:, None] - np.arange(rows_q)[None, :]
        offs = np.asarray(offs)[None, :]
        valid.append((rel >= offs) & (rel < offs + kh))
        row_sel.append((rel + NA_KH - 1)[..., None] == np.arange(nrow))
    row_sel = np.stack(row_sel).astype(np.float32)
    keep = np.stack(valid)[:, :, None, None, :, None] & col_in.T[None, None, :, None, None, :]
    pairs = rpb.reshape(rpb.shape[0] // 2, 2, nrow, ncol).astype(F32)
    tbl = jnp.einsum('vxar,pgrc,cyq->pvxygaq', row_sel, pairs, col_sel, precision=lax.Precision.HIGHEST)
    tbl = jnp.where(keep[None], tbl * LOG2E, NEG_BIG)
    return tbl.reshape(rpb.shape[0] // 2, len(variants), win * GRID_W, 2 * rows_q * GRID_W)


def _outmlp_kernel(*refs, tile_off, final):
    (a_ref, b_ref, c_ref, d_ref, mlat_ref, mctx_ref, g2_ref, wo_ref, w1_ref, w2_ref, gf_ref,
     o_ref) = refs[-12:]
    j = pl.program_id(1) + tile_off
    mod = jnp.where(j == 0, mctx_ref[0], mlat_ref[0])
    cat = jnp.concatenate([a_ref[0], b_ref[0], c_ref[0], d_ref[0]], axis=-1).astype(BF16)
    x1 = _read_tokens(refs[:-12], j == 0) + mod[2:3] * _dot(cat, wo_ref[...])
    ms = jnp.mean(x1 * x1, axis=-1, keepdims=True)
    h2 = ((x1 * lax.rsqrt(ms + EPS) * g2_ref[...]) * (1.0 + mod[4:5]) + mod[3:4]).astype(BF16)
    hid = w1_ref.shape[1]
    step = 1024
    acc = jnp.zeros(x1.shape, F32)
    nchunk = hid // step
    up = _dot(h2, w1_ref[:, 0:step])
    for c in range(nchunk):
        u = jnp.maximum(up, 0.0)
        if c + 1 < nchunk:
            up = _dot(h2, w1_ref[:, (c + 1) * step:(c + 2) * step])
        acc = acc + _dot((u * u).astype(BF16), w2_ref[c * step:(c + 1) * step, :])
    x2 = x1 + mod[5:6] * acc
    if final:
        ms2 = jnp.mean(x2 * x2, axis=-1, keepdims=True)
        x2 = x2 * lax.rsqrt(ms2 + EPS) * gf_ref[...]
    o_ref[0] = x2


def _outmlp(x_ctx, x_lat, a, bm, c, dd, mods_l, g2, wo, w1, w2, gf, final):
    b, t, _ = a.shape
    d = x_lat.shape[-1]
    tile_off = CTX_LEN // ROW_TILE if final else 0
    nt = t // ROW_TILE - tile_off
    ctx_row = mods_l.shape[0] - 1
    full2 = lambda bb, j: (0, 0)
    grp = pl.BlockSpec((1, ROW_TILE, GROUP_WIDTH), lambda bb, j: (bb, j + tile_off, 0))
    tokens, token_specs = _token_sources(x_ctx, x_lat, tile_off)
    return pl.pallas_call(
        functools.partial(_outmlp_kernel, tile_off=tile_off, final=final),
        grid=(b, nt),
        in_specs=token_specs + [
            grp, grp, grp, grp,
            pl.BlockSpec((1, 6, d), lambda bb, j: (bb, 0, 0)),
            pl.BlockSpec((1, 6, d), lambda bb, j: (ctx_row, 0, 0)),
            pl.BlockSpec((1, d), full2),
            pl.BlockSpec(wo.shape, full2),
            pl.BlockSpec(w1.shape, full2),
            pl.BlockSpec(w2.shape, full2),
            pl.BlockSpec((1, d), full2),
        ],
        out_specs=pl.BlockSpec((1, ROW_TILE, d), lambda bb, j: (bb, j, 0)),
        out_shape=jax.ShapeDtypeStruct((b, nt * ROW_TILE, d), F32),
        compiler_params=pltpu.CompilerParams(
            dimension_semantics=("parallel", "arbitrary"), vmem_limit_bytes=VMEM_LIMIT_BYTES),
        name="out_projection_mlp",
    )(*tokens, a, bm, c, dd, mods_l, mods_l, g2, wo, w1, w2, gf)


def _pack_in_projection(w):
    gw, kvw = GROUP_WIDTH, KV_HEADS * HEAD_DIM
    g0 = 9 * gw
    c0 = g0 + 4 * GROUP_HEADS
    d0 = c0 + gw + 2 * kvw

    def pair_swapped(cols):
        n = cols.shape[1]
        return cols.reshape(cols.shape[0], n // 2, 2)[:, :, ::-1].reshape(cols.shape[0], n)

    q, k = w[:, c0:c0 + gw], w[:, c0 + gw:c0 + gw + kvw]
    parts = [w[:, 0:g0], w[:, g0:c0], jnp.zeros((w.shape[0], G_W - 4 * GROUP_HEADS), w.dtype),
             q, pair_swapped(q), k, pair_swapped(k), w[:, c0 + gw + kvw:d0], w[:, d0:d0 + 3 * gw]]
    packed = jnp.concatenate(parts, axis=1).astype(BF16)
    assert packed.shape[1] == IN_PACKED
    return packed


def _rope_tables(n_tokens):
    t = np.arange(n_tokens)
    row = (t // GRID_W).astype(np.float32)
    col = (t % GRID_W).astype(np.float32)
    axis_dims = HEAD_DIM // 2
    inv = np.power(np.float32(ROPE_THETA),
                   (-2.0 * np.arange(axis_dims // 2, dtype=np.float32) / axis_dims).astype(np.float32))
    ang = np.concatenate([row[:, None] * inv, col[:, None] * inv], axis=-1).astype(np.float32)
    cos = np.repeat(np.cos(ang).astype(np.float32), 2, axis=-1)
    sign = np.where(np.arange(HEAD_DIM) % 2 == 0, -1.0, 1.0).astype(np.float32)
    sin = np.repeat(np.sin(ang).astype(np.float32), 2, axis=-1) * sign
    cos = np.concatenate([np.ones((CTX_LEN, HEAD_DIM), np.float32), cos], axis=0)
    sin = np.concatenate([np.zeros((CTX_LEN, HEAD_DIM), np.float32), sin], axis=0)
    return jnp.asarray(np.tile(cos, (1, GROUP_HEADS))), jnp.asarray(np.tile(sin, (1, GROUP_HEADS)))


def kernel(x, c, ctx, c_ctx, w_mod, b_mod, norm1_g, norm2_g, w_in, hgrn_lb_logits, hgrn_norm_g,
           mlstm_gate_b, mlstm_norm_g, gqa_qnorm_g, gqa_knorm_g, na_rpb, w_out, w_mlp1, w_mlp2,
           final_norm_g):
    bsz, seq, d = x.shape
    depth = w_mod.shape[0]
    rows = seq // GRID_W

    n_c = bsz + 1
    pad = (-n_c) % 8
    cc = jnp.concatenate([c, c_ctx[None, :], jnp.zeros((pad, d), F32)], axis=0)
    mods = _modulation(cc, w_mod, b_mod)[:, :n_c].reshape(depth, n_c, 6, d)

    cos4, sin4 = _rope_tables(seq)
    swap = np.arange(HEAD_DIM) ^ 1

    x_ctx, x_lat = ctx, x
    out = None
    for l in range(depth):
        need_ctx = l < depth - 1
        w_packed = _pack_in_projection(w_in[l])
        gq = jnp.tile(gqa_qnorm_g[l], GROUP_HEADS)[None, :]
        gqs = jnp.tile(gqa_qnorm_g[l][swap], GROUP_HEADS)[None, :]
        gk = jnp.tile(gqa_knorm_g[l], KV_HEADS)[None, :]
        gks = jnp.tile(gqa_knorm_g[l][swap], KV_HEADS)[None, :]
        za, zb, zg, qc, kc, vc, qd, kd, vd = _inproj(
            x_ctx, x_lat, mods[l], norm1_g[l][None, :], w_packed, cos4, sin4, gq, gqs, gk, gks)

        a = _hgrn(za, hgrn_lb_logits, jnp.tile(hgrn_norm_g[l], GROUP_HEADS)[None, :], l, need_ctx)
        gate_b = jnp.concatenate([mlstm_gate_b[l], jnp.zeros((G_W - 4 * GROUP_HEADS,), F32)])[None, :]
        bm = _mlstm(zb, zg, gate_b, jnp.tile(mlstm_norm_g[l], GROUP_HEADS)[None, :], need_ctx)
        cg = _gqa(qc, kc, vc, need_ctx)
        kh, win, blocks, variants = _na_plan(rows)
        dg = _na(qd, kd, vd, _na_bias_table(na_rpb[l], kh, win, variants), blocks, need_ctx)

        res = _outmlp(x_ctx, x_lat, a, bm, cg, dg, mods[l], norm2_g[l][None, :], w_out[l].astype(BF16),
                      w_mlp1[l].astype(BF16), w_mlp2[l].astype(BF16), final_norm_g[None, :],
                      final=not need_ctx)
        if need_ctx:
            x_ctx = x_lat = res
        else:
            out = res
    return out
```
